```python
import jax, jax.numpy as jnp
from jax import lax
import numpy as np

D_MODEL = 4096
BATCH = 1
SEQ = 16384
DEPTH = 1
DEC_BATCH = 4
DEC_SEQ = 4096
PAST_LEN = 128

HEAD_DIM = 128
N_ATTN_HEADS = 16
ATTN_WIDTH = N_ATTN_HEADS * HEAD_DIM
N_GDN_HEADS = 16
GDN_WIDTH = N_GDN_HEADS * HEAD_DIM
MIX_WIDTH = ATTN_WIDTH + GDN_WIDTH
IN_COLS = 3 * ATTN_WIDTH + 4 * GDN_WIDTH + 4 * N_GDN_HEADS
DILATED_BRANCHES = ((128, 1), (512, 4), (2048, 16))
Q_BLOCK = 128
ROPE_THETA = 10000.0
CONV_WIDTH = 5
GDN_CHUNK = 64
N_EXPERTS = 32
TOP_K = 4
D_FF = D_MODEL
SWIGLU_LIMIT = 7.0
SWIGLU_ALPHA = 1.702
MOE_BLOCK = 256
EPS = 1e-6
NEG_INF = -1e30

kernel_name = "hymba_dilated_attn_gdn_moe_encoder"

F32 = jnp.float32


def rmsnorm(x, g):
    xf = x.astype(F32)
    y = xf * lax.rsqrt(jnp.mean(xf * xf, axis=-1, keepdims=True) + EPS)
    return (y * g.astype(F32)).astype(x.dtype)


def l2norm(x):
    return x * lax.rsqrt(jnp.sum(x * x, axis=-1, keepdims=True) + EPS)


def rope(x):
    S = x.shape[1]
    half = HEAD_DIM // 2
    inv = ROPE_THETA ** (-jnp.arange(0, HEAD_DIM, 2, dtype=F32) / HEAD_DIM)
    ang = jnp.arange(S, dtype=F32)[:, None] * inv[None, :]
    cos = jnp.cos(ang)[None, :, None, :]
    sin = jnp.sin(ang)[None, :, None, :]
    xf = x.astype(F32)
    x1, x2 = xf[..., :half], xf[..., half:]
    out = jnp.concatenate([x1 * cos - x2 * sin, x2 * cos + x1 * sin], axis=-1)
    return out.astype(x.dtype)


def dilated_attention(q, k, v):
    B, S, H, Dh = q.shape
    q = rope(q)
    k = rope(k)
    scale = Dh ** -0.5
    n_blocks = S // Q_BLOCK

    def block(i):
        start = i * Q_BLOCK
        qb = lax.dynamic_slice_in_dim(q, start, Q_BLOCK, axis=1).astype(F32)
        pos = start + jnp.arange(Q_BLOCK)
        outs, lses = [], []
        for window, dil in DILATED_BRANCHES:
            n_side = window // 2 // dil
            offs = jnp.arange(-n_side, n_side + 1) * dil
            kpos = pos[:, None] + offs[None, :]
            valid = (kpos >= 0) & (kpos < S)
            kidx = jnp.clip(kpos, 0, S - 1)
            kb = jnp.take(k, kidx, axis=1).astype(F32)
            vb = jnp.take(v, kidx, axis=1).astype(F32)
            s = jnp.einsum('bqhd,bqjhd->bhqj', qb, kb) * scale
            s = jnp.where(valid[None, None], s, NEG_INF)
            m = jnp.max(s, axis=-1, keepdims=True)
            p = jnp.exp(s - m)
            l = jnp.sum(p, axis=-1, keepdims=True)
            outs.append(jnp.einsum('bhqj,bqjhd->bhqd', p, vb) / l)
            lses.append(m + jnp.log(l))
        w = jax.nn.softmax(jnp.stack(lses, axis=0), axis=0)
        o = jnp.sum(w * jnp.stack(outs, axis=0), axis=0)
        return o.transpose(0, 2, 1, 3)

    out = lax.map(block, jnp.arange(n_blocks))
    return out.transpose(1, 0, 2, 3, 4).reshape(B, S, H, Dh).astype(q.dtype)


def short_conv(x, w):
    C = x.shape[-1]
    pad = CONV_WIDTH // 2
    return lax.conv_general_dilated(
        x, w[:, None, :].astype(x.dtype), window_strides=(1,), padding=[(pad, pad)],
        dimension_numbers=('NWC', 'WIO', 'NWC'), feature_group_count=C)


def gated_delta_rule(q, k, v, g, beta):
    B, S, H, D = q.shape
    C = GDN_CHUNK
    N = S // C
    q = q * D ** -0.5

    def ch4(t):
        return t.reshape(B, N, C, H, D).transpose(0, 3, 1, 2, 4)

    def ch3(t):
        return t.reshape(B, N, C, H).transpose(0, 3, 1, 2)

    qc, kc, vc = ch4(q), ch4(k), ch4(v)
    bc = ch3(beta)
    gc = jnp.cumsum(ch3(g), axis=-1)
    tri = jnp.tril(jnp.ones((C, C), dtype=bool))
    strict = jnp.tril(jnp.ones((C, C), dtype=bool), -1)
    diff = gc[..., :, None] - gc[..., None, :]
    decay = jnp.where(tri, jnp.exp(jnp.where(tri, diff, 0.0)), 0.0)
    kb = kc * bc[..., None]
    vb = vc * bc[..., None]
    L = jnp.where(strict, jnp.einsum('bhnid,bhnjd->bhnij', kb, kc) * decay, 0.0)
    eye = jnp.eye(C, dtype=F32)
    T = lax.linalg.triangular_solve(eye + L, jnp.broadcast_to(eye, L.shape),
                                    left_side=True, lower=True, unit_diagonal=True)
    u = jnp.einsum('bhnij,bhnjd->bhnid', T, vb)
    w = jnp.einsum('bhnij,bhnjd->bhnid', T, kb * jnp.exp(gc)[..., None])
    attn = jnp.where(tri, jnp.einsum('bhnid,bhnjd->bhnij', qc, kc) * decay, 0.0)

    def to_front(t):
        return jnp.moveaxis(t, 2, 0)

    xs = (to_front(qc), to_front(kc), to_front(u), to_front(w), to_front(gc), to_front(attn))

    def step(state, inp):
        qn, kn, un, wn, gn, an = inp
        v_new = un - jnp.einsum('bhcd,bhde->bhce', wn, state)
        o = (jnp.einsum('bhcd,bhde->bhce', qn * jnp.exp(gn)[..., None], state)
             + jnp.einsum('bhij,bhje->bhie', an, v_new))
        glast = gn[..., -1:]
        state = (state * jnp.exp(glast)[..., None]
                 + jnp.einsum('bhcd,bhce->bhde', kn * jnp.exp(glast - gn)[..., None], v_new))
        return state, o

    state0 = jnp.zeros((B, H, D, D), F32)
    _, o = lax.scan(step, state0, xs)
    return o.transpose(1, 0, 3, 2, 4).reshape(B, S, H, D)


def gated_deltanet(qkv, z, b_fwd, a_fwd, b_bwd, a_bwd, conv_w,
                   a_log_fwd, dt_bias_fwd, a_log_bwd, dt_bias_bwd, norm_w):
    B, S, _ = qkv.shape
    qkv = jax.nn.silu(short_conv(qkv, conv_w)).astype(F32)
    q, k, v = [t.reshape(B, S, N_GDN_HEADS, HEAD_DIM) for t in jnp.split(qkv, 3, axis=-1)]
    q = l2norm(q)
    k = l2norm(k)

    def one_direction(q, k, v, b_logit, a_logit, a_log, dt_bias):
        beta = jax.nn.sigmoid(b_logit.astype(F32))
        g = -jnp.exp(a_log.astype(F32)) * jax.nn.softplus(a_logit.astype(F32) + dt_bias.astype(F32))
        return gated_delta_rule(q, k, v, g, beta)

    def rev(t):
        return t[:, ::-1]

    o = (one_direction(q, k, v, b_fwd, a_fwd, a_log_fwd, dt_bias_fwd)
         + rev(one_direction(rev(q), rev(k), rev(v), rev(b_bwd), rev(a_bwd), a_log_bwd, dt_bias_bwd)))
    o = o * lax.rsqrt(jnp.mean(o * o, axis=-1, keepdims=True) + EPS) * norm_w.astype(F32)
    o = o * jax.nn.silu(z.astype(F32).reshape(B, S, N_GDN_HEADS, HEAD_DIM))
    return o.reshape(B, S, GDN_WIDTH).astype(z.dtype)


def moe(x, w_router, b_router, w_gate, b_gate, w_up, b_up, w_down, b_down):
    B, S, D = x.shape
    T = B * S
    xf = x.reshape(T, D)
    logits = (xf @ w_router + b_router).astype(F32)
    top_vals, top_idx = lax.top_k(logits, TOP_K)
    gates = jax.nn.softmax(top_vals, axis=-1)
    n_assign = T * TOP_K
    flat_e = top_idx.reshape(-1)
    flat_tok = jnp.arange(n_assign, dtype=jnp.int32) // TOP_K
    flat_w = gates.reshape(-1)
    order = jnp.argsort(flat_e)
    sorted_e = flat_e[order]
    counts = jnp.bincount(flat_e, length=N_EXPERTS)
    padded = ((counts + MOE_BLOCK - 1) // MOE_BLOCK) * MOE_BLOCK
    start = jnp.cumsum(counts) - counts
    pend = jnp.cumsum(padded)
    pstart = pend - padded
    rank = jnp.arange(n_assign, dtype=jnp.int32) - start[sorted_e]
    dest = pstart[sorted_e] + rank
    n_blocks = -(-n_assign // MOE_BLOCK) + N_EXPERTS
    n_rows = n_blocks * MOE_BLOCK
    row_tok = jnp.full((n_rows,), T, dtype=jnp.int32).at[dest].set(flat_tok[order])
    row_w = jnp.zeros((n_rows,), F32).at[dest].set(flat_w[order])
    block_e = jnp.minimum(
        jnp.searchsorted(pend, jnp.arange(n_blocks) * MOE_BLOCK, side='right'), N_EXPERTS - 1)
    x_pad = jnp.concatenate([xf, jnp.zeros((1, D), xf.dtype)], axis=0)
    rows = x_pad[row_tok].reshape(n_blocks, MOE_BLOCK, D)

    def expert_block(args):
        xb, e = args
        gate = jnp.minimum(xb @ w_gate[e] + b_gate[e], SWIGLU_LIMIT)
        up = jnp.clip(xb @ w_up[e] + b_up[e], -SWIGLU_LIMIT, SWIGLU_LIMIT)
        glu = gate * jax.nn.sigmoid(gate * SWIGLU_ALPHA)
        return ((up + 1.0) * glu) @ w_down[e] + b_down[e]

    out = lax.map(expert_block, (rows, block_e)).reshape(n_rows, D)
    y = jax.ops.segment_sum(out * row_w[:, None].astype(out.dtype), row_tok, num_segments=T + 1)[:T]
    return y.reshape(B, S, D).astype(x.dtype)


def encoder(x, norm_mix, w_in, conv_w, a_log_fwd, dt_bias_fwd, a_log_bwd, dt_bias_bwd,
            gdn_norm, w_out, norm_ffn, w_router, b_router, w_gate, b_gate, w_up, b_up,
            w_down, b_down, norm_final):
    B, S, _ = x.shape
    sizes = [ATTN_WIDTH] * 3 + [3 * GDN_WIDTH, GDN_WIDTH] + [N_GDN_HEADS] * 4
    split_idx = np.cumsum(sizes)[:-1].tolist()
    for l in range(DEPTH):
        h = rmsnorm(x, norm_mix[l])
        proj = h @ w_in[l]
        aq, ak, av, gqkv, z, b_f, a_f, b_b, a_b = jnp.split(proj, split_idx, axis=-1)
        heads = lambda t: t.reshape(B, S, N_ATTN_HEADS, HEAD_DIM)
        o_attn = dilated_attention(heads(aq), heads(ak), heads(av)).reshape(B, S, ATTN_WIDTH)
        o_gdn = gated_deltanet(gqkv, z, b_f, a_f, b_b, a_b, conv_w[l], a_log_fwd[l], dt_bias_fwd[l],
                               a_log_bwd[l], dt_bias_bwd[l], gdn_norm[l])
        mixed = jnp.concatenate([o_attn.astype(x.dtype), o_gdn.astype(x.dtype)], axis=-1) @ w_out[l]
        x = x + mixed
        x = x + moe(rmsnorm(x, norm_ffn[l]), w_router[l], b_router[l], w_gate[l], b_gate[l],
                    w_up[l], b_up[l], w_down[l], b_down[l])
    return rmsnorm(x, norm_final)


def setup_inputs(seed: int = 0) -> dict:
    key = jax.random.key(seed)
    ks = jax.random.split(key, 24)
    nrm = lambda k, shape, s: jax.random.normal(k, shape, F32) * s
    gain = lambda k, shape: 1.0 + 0.01 * jax.random.normal(k, shape, F32)
    dt = jnp.exp(jax.random.uniform(ks[5], (DEPTH, 2, N_GDN_HEADS), F32,
                                    np.log(0.001), np.log(0.1)))
    dt_bias = dt + jnp.log(-jnp.expm1(-dt))
    a_log = jnp.log(jax.random.uniform(ks[6], (DEPTH, 2, N_GDN_HEADS), F32, 1.0, 16.0))
    return {
        "x_prompt": nrm(ks[0], (BATCH, SEQ, D_MODEL), 1.0),
        "x_sample": nrm(ks[1], (DEC_BATCH, DEC_SEQ, D_MODEL), 1.0),
        "norm_mix": gain(ks[2], (DEPTH, D_MODEL)),
        "w_in": nrm(ks[3], (DEPTH, D_MODEL, IN_COLS), D_MODEL ** -0.5),
        "conv_w": nrm(ks[4], (DEPTH, CONV_WIDTH, 3 * GDN_WIDTH), CONV_WIDTH ** -0.5),
        "a_log_fwd": a_log[:, 0],
        "dt_bias_fwd": dt_bias[:, 0],
        "a_log_bwd": a_log[:, 1],
        "dt_bias_bwd": dt_bias[:, 1],
        "gdn_norm": gain(ks[7], (DEPTH, HEAD_DIM)),
        "w_out": nrm(ks[8], (DEPTH, MIX_WIDTH, D_MODEL), MIX_WIDTH ** -0.5),
        "norm_ffn": gain(ks[9], (DEPTH, D_MODEL)),
        "w_router": nrm(ks[10], (DEPTH, D_MODEL, N_EXPERTS), D_MODEL ** -0.5),
        "b_router": nrm(ks[11], (DEPTH, N_EXPERTS), 0.01),
        "w_gate": nrm(ks[12], (DEPTH, N_EXPERTS, D_MODEL, D_FF), D_MODEL ** -0.5),
        "b_gate": nrm(ks[13], (DEPTH, N_EXPERTS, D_FF), 0.01),
        "w_up": nrm(ks[14], (DEPTH, N_EXPERTS, D_MODEL, D_FF), D_MODEL ** -0.5),
        "b_up": nrm(ks[15], (DEPTH, N_EXPERTS, D_FF), 0.01),
        "w_down": nrm(ks[16], (DEPTH, N_EXPERTS, D_FF, D_MODEL), D_FF ** -0.5),
        "b_down": nrm(ks[17], (DEPTH, N_EXPERTS, D_MODEL), 0.01),
        "norm_final": gain(ks[18], (D_MODEL,)),
    }


def reference(x_prompt, x_sample, norm_mix, w_in, conv_w, a_log_fwd, dt_bias_fwd, a_log_bwd,
              dt_bias_bwd, gdn_norm, w_out, norm_ffn, w_router, b_router, w_gate, b_gate,
              w_up, b_up, w_down, b_down, norm_final):
    y_prompt = encoder(x_prompt, norm_mix, w_in, conv_w, a_log_fwd, dt_bias_fwd, a_log_bwd,
                       dt_bias_bwd, gdn_norm, w_out, norm_ffn, w_router, b_router, w_gate,
                       b_gate, w_up, b_up, w_down, b_down, norm_final)
    y_sample = encoder(x_sample, norm_mix, w_in, conv_w, a_log_fwd, dt_bias_fwd, a_log_bwd,
                       dt_bias_bwd, gdn_norm, w_out, norm_ffn, w_router, b_router, w_gate,
                       b_gate, w_up, b_up, w_down, b_down, norm_final)
    return (y_prompt, y_sample)
```

```python
import functools

import numpy as np
import jax
import jax.numpy as jnp
from jax import lax
from jax.experimental import pallas as pl
from jax.experimental.pallas import tpu as pltpu

F32 = jnp.float32
BF16 = jnp.bfloat16
I32 = jnp.int32

HEAD_DIM = 128
DILATED_BRANCHES = ((128, 1), (512, 4), (2048, 16))
ATTN_SIDE = 64
ATTN_CHUNK = 2 * ATTN_SIDE
ROPE_THETA = 10000.0
GDN_CHUNK = 64
TOP_K = 4
SWIGLU_LIMIT = 7.0
SWIGLU_ALPHA = 1.702
EPS = 1e-6
NEG_INF = -1e30
LANES = 128
BF16_SUBLANES = 16
VMEM_LIMIT = 56 * 1024 * 1024
MOE_ROW_BLOCK = 1024
COMBINE_TOKEN_BLOCK = 128
ATTN_Q_BLOCK = 2048


def _pick(n, cands):
    for c in cands:
        if n % c == 0:
            return c
    raise ValueError(f"no tile in {cands} divides {n}")


def _params(sem, vmem=VMEM_LIMIT):
    return pltpu.CompilerParams(dimension_semantics=sem, vmem_limit_bytes=vmem)


def _rms(x, g):
    ms = jnp.mean(x * x, axis=-1, keepdims=True)
    return x * lax.rsqrt(ms + EPS) * g


def _norm2_kernel(xp_ref, xs_ref, g_ref, o_ref, *, nbp):
    i = pl.program_id(0)

    @pl.when(i < nbp)
    def _():
        o_ref[...] = _rms(xp_ref[...], g_ref[...]).astype(o_ref.dtype)

    @pl.when(i >= nbp)
    def _():
        o_ref[...] = _rms(xs_ref[...], g_ref[...]).astype(o_ref.dtype)


def norm_two_streams(xp, xs, g):
    tp, d = xp.shape
    ts = xs.shape[0]
    tr = _pick(np.gcd(tp, ts), (256, 128, 64, 32, 16))
    nbp = tp // tr
    nb = nbp + ts // tr
    return pl.pallas_call(
        functools.partial(_norm2_kernel, nbp=nbp),
        out_shape=jax.ShapeDtypeStruct((tp + ts, d), BF16),
        grid=(nb,),
        in_specs=[
            pl.BlockSpec((tr, d), lambda i: (jnp.minimum(i, nbp - 1), 0)),
            pl.BlockSpec((tr, d), lambda i: (jnp.maximum(i - nbp, 0), 0)),
            pl.BlockSpec((1, d), lambda i: (0, 0)),
        ],
        out_specs=pl.BlockSpec((tr, d), lambda i: (i, 0)),
        compiler_params=_params(("arbitrary",)),
        name="norm_mix",
    )(xp, xs, g.reshape(1, d))


def _inproj_kernel(h_ref, w_ref, wt_ref, o_ref, ab_ref):
    j = pl.program_id(1)
    o_ref[...] = jnp.dot(h_ref[...], w_ref[...], preferred_element_type=F32).astype(o_ref.dtype)

    @pl.when(j == 0)
    def _():
        ab_ref[...] = jnp.dot(h_ref[...], wt_ref[...], preferred_element_type=F32)


def in_projection(h, w_main, w_tail):
    t, d = h.shape
    n = w_main.shape[1]
    tm = _pick(t, (1024, 512, 256, 128))
    tn = _pick(n, (512, 256, 128))
    return pl.pallas_call(
        _inproj_kernel,
        out_shape=(jax.ShapeDtypeStruct((t, n), BF16), jax.ShapeDtypeStruct((t, LANES), F32)),
        grid=(t // tm, n // tn),
        in_specs=[
            pl.BlockSpec((tm, d), lambda i, j: (i, 0)),
            pl.BlockSpec((d, tn), lambda i, j: (0, j)),
            pl.BlockSpec((d, LANES), lambda i, j: (0, 0)),
        ],
        out_specs=(
            pl.BlockSpec((tm, tn), lambda i, j: (i, j)),
            pl.BlockSpec((tm, LANES), lambda i, j: (i, 0)),
        ),
        compiler_params=_params(("arbitrary", "arbitrary")),
        name="in_proj",
    )(h, w_main, w_tail)


def _rope_kernel(x_ref, cos_ref, sin_ref, o_ref, *, n_heads):
    cos = cos_ref[...]
    sin = sin_ref[...]
    for h in range(n_heads):
        sl = slice(h * HEAD_DIM, (h + 1) * HEAD_DIM)
        x = x_ref[:, sl].astype(F32)
        rot = pltpu.roll(x, HEAD_DIM // 2, axis=1)
        o_ref[:, sl] = (x * cos + rot * sin).astype(o_ref.dtype)


def rope_qk(proj, cos, sin, aw):
    t = proj.shape[0]
    tr = _pick(t, (1024, 512, 256))
    return pl.pallas_call(
        functools.partial(_rope_kernel, n_heads=aw // HEAD_DIM),
        out_shape=jax.ShapeDtypeStruct((t, 2 * aw), BF16),
        grid=(t // tr, 2),
        in_specs=[
            pl.BlockSpec((tr, aw), lambda i, j: (i, j)),
            pl.BlockSpec((tr, HEAD_DIM), lambda i, j: (i, 0)),
            pl.BlockSpec((tr, HEAD_DIM), lambda i, j: (i, 0)),
        ],
        out_specs=pl.BlockSpec((tr, aw), lambda i, j: (i, j)),
        compiler_params=_params(("arbitrary", "arbitrary")),
        name="rope_qk",
    )(proj, cos, sin)


def _attn_kernel(meta_ref, q_ref, kc_ref, kp_ref, kn_ref, vc_ref, vp_ref, vn_ref, o_ref,
                 qf, kw, vw, ob, lb, *, tq, halo):
    i = pl.program_id(0)
    pos0 = meta_ref[0, i]
    slen = meta_ref[1, i]
    qf[...] = q_ref[...].astype(F32)
    kw[0:halo] = kp_ref[...].astype(F32)
    kw[halo:halo + tq] = kc_ref[...].astype(F32)
    kw[halo + tq:] = kn_ref[...].astype(F32)
    vw[0:halo] = vp_ref[...].astype(F32)
    vw[halo:halo + tq] = vc_ref[...].astype(F32)
    vw[halo + tq:] = vn_ref[...].astype(F32)

    scale = HEAD_DIM ** -0.5
    nk = 2 * ATTN_CHUNK
    a_idx = lax.broadcasted_iota(I32, (ATTN_CHUNK, nk), 0)
    j_idx = lax.broadcasted_iota(I32, (ATTN_CHUNK, nk), 1)
    rel = j_idx - a_idx
    band = (rel >= 0) & (rel <= 2 * ATTN_SIDE)

    for b, (_, dil) in enumerate(DILATED_BRANCHES):
        def body(c, carry, b=b, dil=dil):
            r = c % dil
            start = r + (c // dil) * (dil * ATTN_CHUNK)
            q = qf[pl.ds(start, ATTN_CHUNK, stride=dil), :]
            kstart = halo + start - ATTN_SIDE * dil
            k = kw[pl.ds(kstart, nk, stride=dil), :]
            v = vw[pl.ds(kstart, nk, stride=dil), :]
            s = lax.dot_general(q.astype(BF16), k.astype(BF16), (((1,), (1,)), ((), ())),
                                preferred_element_type=F32) * scale
            kpos = (pos0 + start) + (j_idx - ATTN_SIDE) * dil
            valid = band & (kpos >= 0) & (kpos < slen)
            s = jnp.where(valid, s, NEG_INF)
            m = jnp.max(s, axis=-1, keepdims=True)
            p = jnp.exp(s - m)
            l = jnp.sum(p, axis=-1, keepdims=True)
            o = jnp.dot(p.astype(BF16), v.astype(BF16), preferred_element_type=F32) / l
            lse = m + jnp.log(l)
            ob[b, pl.ds(start, ATTN_CHUNK, stride=dil), :] = o
            lb[b, pl.ds(start, ATTN_CHUNK, stride=dil), :] = jnp.broadcast_to(lse, (ATTN_CHUNK, HEAD_DIM))
            return carry

        lax.fori_loop(0, tq // ATTN_CHUNK, body, 0)

    l0, l1, l2 = lb[0], lb[1], lb[2]
    mx = jnp.maximum(jnp.maximum(l0, l1), l2)
    w0, w1, w2 = jnp.exp(l0 - mx), jnp.exp(l1 - mx), jnp.exp(l2 - mx)
    o = (w0 * ob[0] + w1 * ob[1] + w2 * ob[2]) / (w0 + w1 + w2)
    o_ref[...] = o.astype(o_ref.dtype)


def dilated_attention(qk, proj, meta, aw, tq):
    t = qk.shape[0]
    nh = aw // HEAD_DIM
    halo = ATTN_SIDE * max(d for _, d in DILATED_BRANCHES)
    assert tq % halo == 0
    hpb = tq // halo
    nhb = t // halo
    cur = lambda off: (lambda i, h, m: (i, off + h))
    prv = lambda off: (lambda i, h, m: (jnp.maximum(i * hpb - 1, 0), off + h))
    nxt = lambda off: (lambda i, h, m: (jnp.minimum((i + 1) * hpb, nhb - 1), off + h))
    grid_spec = pltpu.PrefetchScalarGridSpec(
        num_scalar_prefetch=1,
        grid=(t // tq, nh),
        in_specs=[
            pl.BlockSpec((tq, HEAD_DIM), cur(0)),
            pl.BlockSpec((tq, HEAD_DIM), cur(nh)),
            pl.BlockSpec((halo, HEAD_DIM), prv(nh)),
            pl.BlockSpec((halo, HEAD_DIM), nxt(nh)),
            pl.BlockSpec((tq, HEAD_DIM), cur(2 * nh)),
            pl.BlockSpec((halo, HEAD_DIM), prv(2 * nh)),
            pl.BlockSpec((halo, HEAD_DIM), nxt(2 * nh)),
        ],
        out_specs=pl.BlockSpec((tq, HEAD_DIM), lambda i, h, m: (i, h)),
        scratch_shapes=[
            pltpu.VMEM((tq, HEAD_DIM), F32),
            pltpu.VMEM((tq + 2 * halo, HEAD_DIM), F32),
            pltpu.VMEM((tq + 2 * halo, HEAD_DIM), F32),
            pltpu.VMEM((3, tq, HEAD_DIM), F32),
            pltpu.VMEM((3, tq, HEAD_DIM), F32),
        ],
    )
    return pl.pallas_call(
        functools.partial(_attn_kernel, tq=tq, halo=halo),
        out_shape=jax.ShapeDtypeStruct((t, aw), BF16),
        grid_spec=grid_spec,
        compiler_params=_params(("arbitrary", "arbitrary")),
        name="dilated_attn",
    )(meta, qk, qk, qk, qk, proj, proj, proj)


def _conv_kernel(meta_ref, xc_ref, xp_ref, xn_ref, w_ref, o_ref, xs, *, tb, hb, tc, n_qk_blocks):
    i = pl.program_id(0)
    j = pl.program_id(1)
    first = meta_ref[0, i] == 1
    last = meta_ref[1, i] == 1
    xs[hb:hb + tb] = xc_ref[...].astype(F32)
    xs[0:hb] = jnp.where(first, 0.0, xp_ref[...].astype(F32))
    xs[hb + tb:] = jnp.where(last, 0.0, xn_ref[...].astype(F32))
    cw = w_ref.shape[0]
    pad = cw // 2
    acc = w_ref[0:1, :] * xs[pl.ds(hb - pad, tb), :]
    for tap in range(1, cw):
        acc = acc + w_ref[tap:tap + 1, :] * xs[pl.ds(hb - pad + tap, tb), :]
    y = acc * (1.0 / (1.0 + jnp.exp(-acc)))
    is_qk = j < n_qk_blocks
    for g in range(tc // HEAD_DIM):
        sl = slice(g * HEAD_DIM, (g + 1) * HEAD_DIM)
        yg = y[:, sl]
        inv = lax.rsqrt(jnp.sum(yg * yg, axis=-1, keepdims=True) + EPS)
        o_ref[:, sl] = yg * jnp.where(is_qk, inv, 1.0)


def gdn_conv(proj, conv_w, meta, col0, gw):
    t = proj.shape[0]
    tb = _pick(t, (512, 256, 128))
    tc = _pick(np.gcd(col0, gw), (512, 256, 128))
    hb = BF16_SUBLANES
    c0 = col0 // tc
    return pl.pallas_call(
        functools.partial(_conv_kernel, tb=tb, hb=hb, tc=tc, n_qk_blocks=2 * gw // tc),
        out_shape=jax.ShapeDtypeStruct((t, 3 * gw), F32),
        grid_spec=pltpu.PrefetchScalarGridSpec(
            num_scalar_prefetch=1,
            grid=(t // tb, 3 * gw // tc),
            in_specs=[
                pl.BlockSpec((tb, tc), lambda i, j, m: (i, c0 + j)),
                pl.BlockSpec((hb, tc), lambda i, j, m: (jnp.maximum(i * (tb // hb) - 1, 0), c0 + j)),
                pl.BlockSpec((hb, tc), lambda i, j, m: (jnp.minimum((i + 1) * (tb // hb), t // hb - 1), c0 + j)),
                pl.BlockSpec((conv_w.shape[0], tc), lambda i, j, m: (0, j)),
            ],
            out_specs=pl.BlockSpec((tb, tc), lambda i, j, m: (i, j)),
            scratch_shapes=[pltpu.VMEM((tb + 2 * hb, tc), F32)],
        ),
        compiler_params=_params(("arbitrary", "arbitrary")),
        name="gdn_conv",
    )(meta, proj, proj, proj, conv_w)


def _cumsum_rows(x, reverse):
    c = x.shape[0]
    row = lax.broadcasted_iota(I32, x.shape, 0)
    s = 1
    while s < c:
        if reverse:
            x = x + jnp.where(row < c - s, pltpu.roll(x, c - s, axis=0), 0.0)
        else:
            x = x + jnp.where(row >= s, pltpu.roll(x, s, axis=0), 0.0)
        s *= 2
    return x


def _gdn_kernel(meta_ref, q_ref, k_ref, v_ref, ab_ref, alog_ref, dtb_ref, *rest,
                n_heads, reverse, finalize):
    if finalize:
        ofwd_ref, z_ref, nw_ref, o_ref, s_ref = rest
    else:
        o_ref, s_ref = rest
    C = GDN_CHUNK
    n_chunks = q_ref.shape[0] // C
    i = pl.program_id(0)

    @pl.when(meta_ref[0, i] == 1)
    def _():
        s_ref[...] = jnp.zeros_like(s_ref)

    ab = ab_ref[...]
    x = ab + dtb_ref[...]
    softplus = jnp.maximum(x, 0.0) + jnp.log(1.0 + jnp.exp(-jnp.abs(x)))
    g_all = -jnp.exp(alog_ref[...]) * softplus
    beta_all = 1.0 / (1.0 + jnp.exp(-ab))
    b_col = 2 * n_heads if reverse else 0
    g_col = b_col + n_heads

    gc_parts = [_cumsum_rows(g_all[c * C:(c + 1) * C], reverse) for c in range(n_chunks)]
    gc_all = jnp.concatenate(gc_parts, axis=0) if n_chunks > 1 else gc_parts[0]
    pad_rows = LANES - gc_all.shape[0]
    gc_sq = gc_all if pad_rows == 0 else jnp.concatenate(
        [gc_all, jnp.zeros((pad_rows, LANES), F32)], axis=0)
    gc_t = gc_sq.T

    ri = lax.broadcasted_iota(I32, (C, C), 0)
    ci = lax.broadcasted_iota(I32, (C, C), 1)
    incl = (ri <= ci) if reverse else (ri >= ci)
    strict = (ri < ci) if reverse else (ri > ci)
    eye = (ri == ci).astype(F32)
    scale = HEAD_DIM ** -0.5
    nt = (((1,), (1,)), ((), ()))
    tn = (((0,), (0,)), ((), ()))
    bdot = lambda a, b: jnp.dot(a.astype(BF16), b.astype(BF16), preferred_element_type=F32)

    order = range(n_chunks - 1, -1, -1) if reverse else range(n_chunks)
    for c in order:
        rows = slice(c * C, (c + 1) * C)
        g_last_row = c * C if reverse else (c + 1) * C - 1
        for h in range(n_heads):
            hs = slice(h * HEAD_DIM, (h + 1) * HEAD_DIM)
            q = q_ref[rows, hs] * scale
            k = k_ref[rows, hs]
            v = v_ref[rows, hs]
            beta = beta_all[rows, b_col + h:b_col + h + 1]
            gcc = gc_all[rows, g_col + h:g_col + h + 1]
            gcr = gc_t[g_col + h:g_col + h + 1, rows]
            glast = gc_all[g_last_row:g_last_row + 1, g_col + h:g_col + h + 1]
            decay = jnp.where(incl, jnp.exp(jnp.where(incl, gcc - gcr, 0.0)), 0.0)
            eg = jnp.exp(gcc)
            kb = k * beta
            vb = v * beta
            a2 = lax.dot_general(jnp.concatenate([kb, q], axis=0).astype(BF16), k.astype(BF16), nt,
                                 preferred_element_type=F32)
            lmat = jnp.where(strict, a2[:C] * decay, 0.0)
            attn = jnp.where(incl, a2[C:] * decay, 0.0)
            tmat = eye - lmat
            lp = lmat
            p = 2
            while p < C:
                lp = bdot(lp, lp)
                tmat = tmat + bdot(tmat, lp)
                p *= 2
            uw = bdot(tmat, jnp.concatenate([vb, kb * eg], axis=1))
            u = uw[:, :HEAD_DIM]
            w = uw[:, HEAD_DIM:]
            state = s_ref[h]
            wq = bdot(jnp.concatenate([w, q * eg], axis=0), state)
            v_new = u - wq[:C]
            o = wq[C:] + bdot(attn, v_new)
            kd = k * jnp.exp(glast - gcc)
            s_ref[h] = state * jnp.exp(glast) + lax.dot_general(
                kd.astype(BF16), v_new.astype(BF16), tn, preferred_element_type=F32)
            if finalize:
                o = o + ofwd_ref[rows, hs]
                o = o * lax.rsqrt(jnp.mean(o * o, axis=-1, keepdims=True) + EPS) * nw_ref[...]
                z = z_ref[rows, hs].astype(F32)
                o = o * (z * (1.0 / (1.0 + jnp.exp(-z))))
            o_ref[rows, hs] = o.astype(o_ref.dtype)


def gdn_scan(qkv, ab, alog_row, dtb_row, meta, gw, reverse, ofwd=None, proj=None, z_col=None, norm_w=None):
    t = qkv.shape[0]
    nh = gw // HEAD_DIM
    rb = LANES
    nb = t // rb
    finalize = ofwd is not None
    idx = (lambda i, m: (nb - 1 - i)) if reverse else (lambda i, m: i)
    row_spec = lambda cb: pl.BlockSpec((rb, gw), lambda i, m: (idx(i, m), cb))
    in_specs = [row_spec(0), row_spec(1), row_spec(2),
                pl.BlockSpec((rb, LANES), lambda i, m: (idx(i, m), 0)),
                pl.BlockSpec((1, LANES), lambda i, m: (0, 0)),
                pl.BlockSpec((1, LANES), lambda i, m: (0, 0))]
    args = [meta, qkv, qkv, qkv, ab, alog_row, dtb_row]
    if finalize:
        assert z_col % gw == 0
        in_specs += [pl.BlockSpec((rb, gw), lambda i, m: (idx(i, m), 0)),
                     pl.BlockSpec((rb, gw), lambda i, m: (idx(i, m), z_col // gw)),
                     pl.BlockSpec((1, HEAD_DIM), lambda i, m: (0, 0))]
        args += [ofwd, proj, norm_w]
    return pl.pallas_call(
        functools.partial(_gdn_kernel, n_heads=nh, reverse=reverse, finalize=finalize),
        out_shape=jax.ShapeDtypeStruct((t, gw), BF16 if finalize else F32),
        grid_spec=pltpu.PrefetchScalarGridSpec(
            num_scalar_prefetch=1,
            grid=(nb,),
            in_specs=in_specs,
            out_specs=pl.BlockSpec((rb, gw), lambda i, m: (idx(i, m), 0)),
            scratch_shapes=[pltpu.VMEM((nh, HEAD_DIM, HEAD_DIM), F32)],
        ),
        compiler_params=_params(("arbitrary",)),
        name="gdn_scan_bwd" if reverse else "gdn_scan_fwd",
    )(*args)


def _outproj_kernel(oa_ref, og_ref, wa_ref, wg_ref, xp_ref, xs_ref, o_ref, *, nbp):
    i = pl.program_id(0)
    mixed = (jnp.dot(oa_ref[...], wa_ref[...], preferred_element_type=F32)
             + jnp.dot(og_ref[...], wg_ref[...], preferred_element_type=F32))

    @pl.when(i < nbp)
    def _():
        o_ref[...] = xp_ref[...] + mixed

    @pl.when(i >= nbp)
    def _():
        o_ref[...] = xs_ref[...] + mixed


def out_projection(oa, og, w_out, xp, xs):
    t, aw = oa.shape
    gw = og.shape[1]
    d = w_out.shape[1]
    tp = xp.shape[0]
    tm = _pick(np.gcd(tp, xs.shape[0]), (1024, 512, 256, 128))
    tn = _pick(d, (512, 256, 128))
    nbp = tp // tm
    assert aw % 16 == 0
    return pl.pallas_call(
        functools.partial(_outproj_kernel, nbp=nbp),
        out_shape=jax.ShapeDtypeStruct((t, d), F32),
        grid=(t // tm, d // tn),
        in_specs=[
            pl.BlockSpec((tm, aw), lambda i, j: (i, 0)),
            pl.BlockSpec((tm, gw), lambda i, j: (i, 0)),
            pl.BlockSpec((aw, tn), lambda i, j: (0, j)),
            pl.BlockSpec((gw, tn), lambda i, j: (aw // gw, j)),
            pl.BlockSpec((tm, tn), lambda i, j: (jnp.minimum(i, nbp - 1), jnp.where(i < nbp, j, 0))),
            pl.BlockSpec((tm, tn), lambda i, j: (jnp.maximum(i - nbp, 0), jnp.where(i >= nbp, j, 0))),
        ],
        out_specs=pl.BlockSpec((tm, tn), lambda i, j: (i, j)),
        compiler_params=_params(("arbitrary", "arbitrary")),
        name="out_proj",
    )(oa, og, w_out, w_out, xp, xs)


def _router_kernel(x_ref, g_ref, wr_ref, br_ref, h_ref, idx_ref, gate_ref):
    h = _rms(x_ref[...], g_ref[...])
    h_ref[...] = h
    logits = jnp.dot(h, wr_ref[...], precision=lax.Precision.HIGHEST,
                     preferred_element_type=F32) + br_ref[...]
    col = lax.broadcasted_iota(I32, logits.shape, 1).astype(F32)
    vals = logits
    tops, idxs = [], []
    for _ in range(TOP_K):
        m = jnp.max(vals, axis=-1, keepdims=True)
        idx = jnp.min(jnp.where(vals == m, col, float(LANES)), axis=-1, keepdims=True)
        tops.append(m)
        idxs.append(idx)
        vals = jnp.where(col == idx, -jnp.inf, vals)
    es = [jnp.exp(v - tops[0]) for v in tops]
    den = es[0]
    for e in es[1:]:
        den = den + e
    idx_out = jnp.zeros(logits.shape, F32)
    gate_out = jnp.zeros(logits.shape, F32)
    for k in range(TOP_K):
        idx_out = jnp.where(col == float(k), idxs[k], idx_out)
        gate_out = jnp.where(col == float(k), es[k] / den, gate_out)
    idx_ref[...] = idx_out.astype(I32)
    gate_ref[...] = gate_out


def ffn_norm_router(x1, g, w_router, b_router):
    t, d = x1.shape
    e = w_router.shape[1]
    assert TOP_K <= e <= LANES
    wr = jnp.zeros((d, LANES), F32).at[:, :e].set(w_router)
    br = jnp.full((1, LANES), NEG_INF, F32).at[0, :e].set(b_router)
    tr = _pick(t, (256, 128))
    return pl.pallas_call(
        _router_kernel,
        out_shape=(jax.ShapeDtypeStruct((t, d), F32),
                   jax.ShapeDtypeStruct((t, LANES), I32),
                   jax.ShapeDtypeStruct((t, LANES), F32)),
        grid=(t // tr,),
        in_specs=[
            pl.BlockSpec((tr, d), lambda i: (i, 0)),
            pl.BlockSpec((1, d), lambda i: (0, 0)),
            pl.BlockSpec((d, LANES), lambda i: (0, 0)),
            pl.BlockSpec((1, LANES), lambda i: (0, 0)),
        ],
        out_specs=(pl.BlockSpec((tr, d), lambda i: (i, 0)),
                   pl.BlockSpec((tr, LANES), lambda i: (i, 0)),
                   pl.BlockSpec((tr, LANES), lambda i: (i, 0))),
        compiler_params=_params(("arbitrary",)),
        name="ffn_norm_router",
    )(x1, g.reshape(1, d), wr, br)


def _gather_rows(tok_ref, src_hbm, dst, sem, n_rows):
    def body(r, carry):
        tok = tok_ref[0, 0, r]
        pltpu.make_async_copy(src_hbm.at[pl.ds(tok, 1)], dst.at[pl.ds(r, 1)], sem).start()
        return carry
    lax.fori_loop(0, n_rows, body, 0)


def _wait_rows(src_hbm, dst, sem):
    pltpu.make_async_copy(src_hbm.at[pl.ds(0, dst.shape[0])], dst, sem).wait()


def _moe_up_kernel(be_ref, nv_ref, tokc_ref, tokn_ref, x_hbm, wg_ref, bg_ref, wu_ref, bu_ref, h_ref,
                   xg, xb, sem, *, tm):
    m = pl.program_id(0)
    n = pl.program_id(1)
    nvalid = nv_ref[0]
    live = m < nvalid

    @pl.when(live & (n == 0))
    def _():
        @pl.when(m == 0)
        def _():
            _gather_rows(tokc_ref, x_hbm, xg, sem, tm)
        _wait_rows(x_hbm, xg, sem)
        xb[...] = xg[...].astype(BF16)

        @pl.when(m + 1 < nvalid)
        def _():
            _gather_rows(tokn_ref, x_hbm, xg, sem, tm)

    @pl.when(live)
    def _():
        x = xb[...]
        gate = jnp.dot(x, wg_ref[...].astype(BF16), preferred_element_type=F32) + bg_ref[...]
        up = jnp.dot(x, wu_ref[...].astype(BF16), preferred_element_type=F32) + bu_ref[...]
        gate = jnp.minimum(gate, SWIGLU_LIMIT)
        up = jnp.clip(up, -SWIGLU_LIMIT, SWIGLU_LIMIT)
        glu = gate * (1.0 / (1.0 + jnp.exp(-gate * SWIGLU_ALPHA)))
        h_ref[...] = ((up + 1.0) * glu).astype(h_ref.dtype)

    @pl.when(jnp.logical_not(live))
    def _():
        h_ref[...] = jnp.zeros_like(h_ref)


def moe_up(h2, row_tok3, block_e, nvalid, w_gate, b_gate, w_up, b_up, tm):
    n_blocks = row_tok3.shape[0]
    e, d, f = w_gate.shape
    tn = _pick(f, (256, 128))
    nt = f // tn
    wmap = lambda m, n, be, nv: (be[m], 0, jnp.where(m < nv[0], n, nt - 1))
    return pl.pallas_call(
        functools.partial(_moe_up_kernel, tm=tm),
        out_shape=jax.ShapeDtypeStruct((n_blocks * tm, f), BF16),
        grid_spec=pltpu.PrefetchScalarGridSpec(
            num_scalar_prefetch=2,
            grid=(n_blocks, nt),
            in_specs=[
                pl.BlockSpec((1, 1, tm), lambda m, n, be, nv: (m, 0, 0), memory_space=pltpu.SMEM),
                pl.BlockSpec((1, 1, tm), lambda m, n, be, nv: (jnp.minimum(m + 1, n_blocks - 1), 0, 0),
                             memory_space=pltpu.SMEM),
                pl.BlockSpec(memory_space=pl.ANY),
                pl.BlockSpec((None, d, tn), wmap),
                pl.BlockSpec((None, 1, tn), wmap),
                pl.BlockSpec((None, d, tn), wmap),
                pl.BlockSpec((None, 1, tn), wmap),
            ],
            out_specs=pl.BlockSpec((tm, tn), lambda m, n, be, nv: (m, n)),
            scratch_shapes=[pltpu.VMEM((tm, d), F32), pltpu.VMEM((tm, d), BF16),
                            pltpu.SemaphoreType.DMA(())],
        ),
        compiler_params=_params(("arbitrary", "arbitrary")),
        name="moe_up",
    )(block_e, nvalid, row_tok3, row_tok3, h2, w_gate, b_gate.reshape(e, 1, f), w_up, b_up.reshape(e, 1, f))


def _moe_down_kernel(be_ref, nv_ref, h_ref, wd_ref, bd_ref, o_ref):
    m = pl.program_id(0)
    live = m < nv_ref[0]

    @pl.when(live)
    def _():
        o_ref[...] = jnp.dot(h_ref[...], wd_ref[...].astype(BF16), preferred_element_type=F32) + bd_ref[...]

    @pl.when(jnp.logical_not(live))
    def _():
        o_ref[...] = jnp.zeros_like(o_ref)


def moe_down(hmid, block_e, nvalid, w_down, b_down, tm):
    n_rows, f = hmid.shape
    e, _, d = w_down.shape
    n_blocks = n_rows // tm
    tn = _pick(d, (256, 128))
    nt = d // tn
    wmap = lambda m, n, be, nv: (be[m], 0, jnp.where(m < nv[0], n, nt - 1))
    return pl.pallas_call(
        _moe_down_kernel,
        out_shape=jax.ShapeDtypeStruct((n_rows, d), F32),
        grid_spec=pltpu.PrefetchScalarGridSpec(
            num_scalar_prefetch=2,
            grid=(n_blocks, nt),
            in_specs=[
                pl.BlockSpec((tm, f), lambda m, n, be, nv: (m, 0)),
                pl.BlockSpec((None, f, tn), wmap),
                pl.BlockSpec((None, 1, tn), wmap),
            ],
            out_specs=pl.BlockSpec((tm, tn), lambda m, n, be, nv: (m, n)),
        ),
        compiler_params=_params(("arbitrary", "arbitrary")),
        name="moe_down",
    )(block_e, nvalid, hmid, w_down, b_down.reshape(e, 1, d))


def _combine_kernel(destc_ref, destn_ref, rows_hbm, x_ref, gate_ref, g_ref, o_ref, buf, sem, *, tr, nb):
    i = pl.program_id(0)
    slot = i % 2
    n_rows = TOP_K * tr

    @pl.when(i == 0)
    def _():
        _gather_rows(destc_ref, rows_hbm, buf.at[0], sem.at[0], n_rows)

    @pl.when(i + 1 < nb)
    def _():
        _gather_rows(destn_ref, rows_hbm, buf.at[1 - slot], sem.at[1 - slot], n_rows)

    _wait_rows(rows_hbm, buf.at[slot], sem.at[slot])
    y = x_ref[...]
    gates = gate_ref[...]
    for k in range(TOP_K):
        y = y + gates[:, k:k + 1] * buf[slot, pl.ds(k * tr, tr), :]
    o_ref[...] = _rms(y, g_ref[...])


def moe_combine_norm(x1, rows, dest3, gates, g, row0, n_tok, tr):
    d = x1.shape[1]
    nb = n_tok // tr
    b0 = row0 // tr
    nbt = dest3.shape[0]
    return pl.pallas_call(
        functools.partial(_combine_kernel, tr=tr, nb=nb),
        out_shape=jax.ShapeDtypeStruct((n_tok, d), F32),
        grid=(nb,),
        in_specs=[
            pl.BlockSpec((1, 1, TOP_K * tr), lambda i: (b0 + i, 0, 0), memory_space=pltpu.SMEM),
            pl.BlockSpec((1, 1, TOP_K * tr), lambda i: (jnp.minimum(b0 + i + 1, nbt - 1), 0, 0),
                         memory_space=pltpu.SMEM),
            pl.BlockSpec(memory_space=pl.ANY),
            pl.BlockSpec((tr, d), lambda i: (b0 + i, 0)),
            pl.BlockSpec((tr, LANES), lambda i: (b0 + i, 0)),
            pl.BlockSpec((1, d), lambda i: (0, 0)),
        ],
        out_specs=pl.BlockSpec((tr, d), lambda i: (i, 0)),
        scratch_shapes=[pltpu.VMEM((2, TOP_K * tr, d), F32), pltpu.SemaphoreType.DMA((2,))],
        compiler_params=_params(("arbitrary",)),
        name="moe_combine_norm",
    )(dest3, dest3, rows, x1, gates, g.reshape(1, d))


def _route(top_idx, n_experts, tm):
    t = top_idx.shape[0]
    n_assign = t * TOP_K
    flat_e = top_idx.reshape(-1)
    onehot = (flat_e[:, None] == jnp.arange(n_experts, dtype=I32)[None, :]).astype(I32)
    csum = jnp.cumsum(onehot, axis=0)
    counts = csum[-1]
    rank = jnp.take_along_axis(csum, flat_e[:, None], axis=1)[:, 0] - 1
    padded = ((counts + tm - 1) // tm) * tm
    pend = jnp.cumsum(padded)
    pstart = pend - padded
    dest = pstart[flat_e] + rank
    n_blocks = -(-n_assign // tm) + n_experts
    flat_tok = jnp.arange(n_assign, dtype=I32) // TOP_K
    row_tok = jnp.zeros((n_blocks * tm,), I32).at[dest].set(flat_tok, unique_indices=True)
    block_e = jnp.minimum(
        jnp.searchsorted(pend, jnp.arange(n_blocks, dtype=I32) * tm, side='right'), n_experts - 1).astype(I32)
    nvalid = (pend[-1] // tm).astype(I32).reshape(1)
    return row_tok, dest.reshape(t, TOP_K).astype(I32), block_e, nvalid


def _seq_table(seqs, blk):
    pos0, slen, first, last = [], [], [], []
    for (b, s) in seqs:
        assert s % blk == 0
        nb = s // blk
        for _ in range(b):
            for j in range(nb):
                pos0.append(j * blk)
                slen.append(s)
                first.append(int(j == 0))
                last.append(int(j == nb - 1))
    return np.array(pos0, np.int32), np.array(slen, np.int32), np.array(first, np.int32), np.array(last, np.int32)


def _rope_tables(seqs):
    half = HEAD_DIM // 2
    inv = ROPE_THETA ** (-jnp.arange(0, HEAD_DIM, 2, dtype=F32) / HEAD_DIM)
    cos_l, sin_l = [], []
    for (b, s) in seqs:
        ang = jnp.arange(s, dtype=F32)[:, None] * inv[None, :]
        c = jnp.cos(ang)
        sn = jnp.sin(ang)
        cos_l.append(jnp.tile(jnp.concatenate([c, c], axis=-1), (b, 1)))
        sin_l.append(jnp.tile(jnp.concatenate([-sn, sn], axis=-1), (b, 1)))
    del half
    return jnp.concatenate(cos_l, axis=0), jnp.concatenate(sin_l, axis=0)


def _layer(xp, xs, seqs, norm_mix, w_in, conv_w, a_log_fwd, dt_bias_fwd, a_log_bwd, dt_bias_bwd,
           gdn_norm, w_out, norm_ffn, w_router, b_router, w_gate, b_gate, w_up, b_up, w_down, b_down,
           norm_final):
    moe_tm, comb_tr, attn_tq = MOE_ROW_BLOCK, COMBINE_TOKEN_BLOCK, ATTN_Q_BLOCK
    d = xp.shape[1]
    nhg = a_log_fwd.shape[0]
    gw = nhg * HEAD_DIM
    aw = w_out.shape[0] - gw
    n_main = 3 * aw + 4 * gw
    assert w_in.shape[1] == n_main + 4 * nhg and 4 * nhg <= LANES
    t = xp.shape[0] + xs.shape[0]

    h = norm_two_streams(xp, xs, norm_mix)
    w_main = w_in[:, :n_main].astype(BF16)
    w_tail = jnp.zeros((d, LANES), BF16).at[:, :4 * nhg].set(w_in[:, n_main:].astype(BF16))
    proj, ab = in_projection(h, w_main, w_tail)

    cos, sin = _rope_tables(seqs)
    qk = rope_qk(proj, cos, sin, aw)
    pos0, slen, _, _ = _seq_table(seqs, attn_tq)
    o_attn = dilated_attention(qk, proj, jnp.asarray(np.stack([pos0, slen])), aw, attn_tq)

    conv_tb = _pick(t, (512, 256, 128))
    _, _, cfirst, clast = _seq_table(seqs, conv_tb)
    qkv = gdn_conv(proj, conv_w, jnp.asarray(np.stack([cfirst, clast])), 3 * aw, gw)
    _, _, gfirst, glast = _seq_table(seqs, LANES)
    zrow = jnp.zeros((nhg,), F32)
    alog_row = jnp.zeros((1, LANES), F32).at[0, :4 * nhg].set(
        jnp.concatenate([zrow, a_log_fwd, zrow, a_log_bwd]))
    dtb_row = jnp.zeros((1, LANES), F32).at[0, :4 * nhg].set(
        jnp.concatenate([zrow, dt_bias_fwd, zrow, dt_bias_bwd]))
    o_fwd = gdn_scan(qkv, ab, alog_row, dtb_row, jnp.asarray(gfirst[None, :]), gw, reverse=False)
    o_gdn = gdn_scan(qkv, ab, alog_row, dtb_row, jnp.asarray(glast[::-1][None, :].copy()), gw, reverse=True,
                     ofwd=o_fwd, proj=proj, z_col=3 * aw + 3 * gw, norm_w=gdn_norm.reshape(1, HEAD_DIM))

    x1 = out_projection(o_attn, o_gdn, w_out.astype(BF16), xp, xs)

    h2, idx_pad, gate_pad = ffn_norm_router(x1, norm_ffn, w_router, b_router)
    n_experts = w_router.shape[1]
    row_tok, dest, block_e, nvalid = _route(idx_pad[:, :TOP_K], n_experts, moe_tm)
    n_blocks = block_e.shape[0]
    hmid = moe_up(h2, row_tok.reshape(n_blocks, 1, moe_tm), block_e, nvalid, w_gate, b_gate, w_up, b_up, moe_tm)
    rows = moe_down(hmid, block_e, nvalid, w_down, b_down, moe_tm)
    dest3 = dest.reshape(t // comb_tr, 1, comb_tr, TOP_K).transpose(0, 1, 3, 2).reshape(t // comb_tr, 1, TOP_K * comb_tr)
    tp = xp.shape[0]
    yp = moe_combine_norm(x1, rows, dest3, gate_pad, norm_final, 0, tp, comb_tr)
    ys = moe_combine_norm(x1, rows, dest3, gate_pad, norm_final, tp, xs.shape[0], comb_tr)
    return yp, ys


def kernel(x_prompt, x_sample, norm_mix, w_in, conv_w, a_log_fwd, dt_bias_fwd, a_log_bwd, dt_bias_bwd,
           gdn_norm, w_out, norm_ffn, w_router, b_router, w_gate, b_gate, w_up, b_up, w_down, b_down,
           norm_final):
    assert norm_mix.shape[0] == 1, "single-layer encoder"
    bp, sp, d = x_prompt.shape
    bs, ss, _ = x_sample.shape
    seqs = ((bp, sp), (bs, ss))
    layer0 = lambda a: a.reshape(a.shape[1:])
    yp, ys = _layer(
        x_prompt.reshape(bp * sp, d), x_sample.reshape(bs * ss, d), seqs,
        *[layer0(a) for a in (norm_mix, w_in, conv_w, a_log_fwd, dt_bias_fwd, a_log_bwd, dt_bias_bwd,
                              gdn_norm, w_out, norm_ffn, w_router, b_router, w_gate, b_gate, w_up,
                              b_up, w_down, b_down)],
        norm_final)
    return yp.reshape(bp, sp, d), ys.reshape(bs, ss, d)
```

```python
import functools

import numpy as np
import jax
import jax.numpy as jnp
from jax import lax
from jax.experimental import pallas as pl
from jax.experimental.pallas import tpu as pltpu

F32 = jnp.float32
BF16 = jnp.bfloat16
I32 = jnp.int32

HEAD_DIM = 128
DILATED_BRANCHES = ((128, 1), (512, 4), (2048, 16))
ATTN_SIDE = 64
ATTN_CHUNK = 2 * ATTN_SIDE
ROPE_THETA = 10000.0
GDN_CHUNK = 64
TOP_K = 4
SWIGLU_LIMIT = 7.0
SWIGLU_ALPHA = 1.702
EPS = 1e-6
NEG_INF = -1e30
LANES = 128
BF16_SUBLANES = 16
VMEM_LIMIT = 56 * 1024 * 1024
MOE_ROW_BLOCK = 1024
COMBINE_TOKEN_BLOCK = 128
ATTN_Q_BLOCK = 2048
GATHER_UNROLL = 8
ATTN_UNROLL = 8


def _pick(n, cands):
    for c in cands:
        if n % c == 0:
            return c
    raise ValueError(f"no tile in {cands} divides {n}")


def _params(sem, vmem=VMEM_LIMIT):
    return pltpu.CompilerParams(dimension_semantics=sem, vmem_limit_bytes=vmem)


def _rms(x, g):
    ms = jnp.mean(x * x, axis=-1, keepdims=True)
    return x * lax.rsqrt(ms + EPS) * g


def _norm2_kernel(xp_ref, xs_ref, g_ref, o_ref, *, nbp):
    i = pl.program_id(0)

    @pl.when(i < nbp)
    def _():
        o_ref[...] = _rms(xp_ref[...], g_ref[...]).astype(o_ref.dtype)

    @pl.when(i >= nbp)
    def _():
        o_ref[...] = _rms(xs_ref[...], g_ref[...]).astype(o_ref.dtype)


def norm_two_streams(xp, xs, g):
    tp, d = xp.shape
    ts = xs.shape[0]
    tr = _pick(np.gcd(tp, ts), (256, 128, 64, 32, 16))
    nbp = tp // tr
    nb = nbp + ts // tr
    return pl.pallas_call(
        functools.partial(_norm2_kernel, nbp=nbp),
        out_shape=jax.ShapeDtypeStruct((tp + ts, d), BF16),
        grid=(nb,),
        in_specs=[
            pl.BlockSpec((tr, d), lambda i: (jnp.minimum(i, nbp - 1), 0)),
            pl.BlockSpec((tr, d), lambda i: (jnp.maximum(i - nbp, 0), 0)),
            pl.BlockSpec((1, d), lambda i: (0, 0)),
        ],
        out_specs=pl.BlockSpec((tr, d), lambda i: (i, 0)),
        compiler_params=_params(("arbitrary",)),
        name="norm_mix",
    )(xp, xs, g.reshape(1, d))


def _inproj_kernel(h_ref, w_ref, wt_ref, o_ref, ab_ref):
    j = pl.program_id(1)
    o_ref[...] = jnp.dot(h_ref[...], w_ref[...], preferred_element_type=F32).astype(o_ref.dtype)

    @pl.when(j == 0)
    def _():
        ab_ref[...] = jnp.dot(h_ref[...], wt_ref[...], preferred_element_type=F32)


def in_projection(h, w_main, w_tail):
    t, d = h.shape
    n = w_main.shape[1]
    tm = _pick(t, (1024, 512, 256, 128))
    tn = _pick(n, (512, 256, 128))
    return pl.pallas_call(
        _inproj_kernel,
        out_shape=(jax.ShapeDtypeStruct((t, n), BF16), jax.ShapeDtypeStruct((t, LANES), F32)),
        grid=(t // tm, n // tn),
        in_specs=[
            pl.BlockSpec((tm, d), lambda i, j: (i, 0)),
            pl.BlockSpec((d, tn), lambda i, j: (0, j)),
            pl.BlockSpec((d, LANES), lambda i, j: (0, 0)),
        ],
        out_specs=(
            pl.BlockSpec((tm, tn), lambda i, j: (i, j)),
            pl.BlockSpec((tm, LANES), lambda i, j: (i, 0)),
        ),
        compiler_params=_params(("arbitrary", "arbitrary")),
        name="in_proj",
    )(h, w_main, w_tail)


def _rope_kernel(x_ref, cos_ref, sin_ref, o_ref, *, n_heads):
    cos = cos_ref[...]
    sin = sin_ref[...]
    for h in range(n_heads):
        sl = slice(h * HEAD_DIM, (h + 1) * HEAD_DIM)
        x = x_ref[:, sl].astype(F32)
        rot = pltpu.roll(x, HEAD_DIM // 2, axis=1)
        o_ref[:, sl] = (x * cos + rot * sin).astype(o_ref.dtype)


def rope_qk(proj, cos, sin, aw):
    t = proj.shape[0]
    tr = _pick(t, (1024, 512, 256))
    return pl.pallas_call(
        functools.partial(_rope_kernel, n_heads=aw // HEAD_DIM),
        out_shape=jax.ShapeDtypeStruct((t, 2 * aw), BF16),
        grid=(t // tr, 2),
        in_specs=[
            pl.BlockSpec((tr, aw), lambda i, j: (i, j)),
            pl.BlockSpec((tr, HEAD_DIM), lambda i, j: (i, 0)),
            pl.BlockSpec((tr, HEAD_DIM), lambda i, j: (i, 0)),
        ],
        out_specs=pl.BlockSpec((tr, aw), lambda i, j: (i, j)),
        compiler_params=_params(("arbitrary", "arbitrary")),
        name="rope_qk",
    )(proj, cos, sin)


def _attn_kernel(meta_ref, q_ref, kc_ref, kp_ref, kn_ref, vc_ref, vp_ref, vn_ref, o_ref,
                 qf, kw, vw, ob, lb, *, tq, halo):
    i = pl.program_id(0)
    pos0 = meta_ref[0, i]
    slen = meta_ref[1, i]
    qf[...] = q_ref[...].astype(F32)
    kw[0:halo] = kp_ref[...].astype(F32)
    kw[halo:halo + tq] = kc_ref[...].astype(F32)
    kw[halo + tq:] = kn_ref[...].astype(F32)
    vw[0:halo] = vp_ref[...].astype(F32)
    vw[halo:halo + tq] = vc_ref[...].astype(F32)
    vw[halo + tq:] = vn_ref[...].astype(F32)

    scale = HEAD_DIM ** -0.5
    nk = 2 * ATTN_CHUNK
    a_idx = lax.broadcasted_iota(I32, (ATTN_CHUNK, nk), 0)
    j_idx = lax.broadcasted_iota(I32, (ATTN_CHUNK, nk), 1)
    rel = j_idx - a_idx
    band = (rel >= 0) & (rel <= 2 * ATTN_SIDE)

    for b, (_, dil) in enumerate(DILATED_BRANCHES):
        def body(c, carry, b=b, dil=dil):
            r = c % dil
            start = r + (c // dil) * (dil * ATTN_CHUNK)
            q = qf[pl.ds(start, ATTN_CHUNK, stride=dil), :]
            kstart = halo + start - ATTN_SIDE * dil
            k = kw[pl.ds(kstart, nk, stride=dil), :]
            v = vw[pl.ds(kstart, nk, stride=dil), :]
            s = lax.dot_general(q.astype(BF16), k.astype(BF16), (((1,), (1,)), ((), ())),
                                preferred_element_type=F32) * scale
            kpos = (pos0 + start) + (j_idx - ATTN_SIDE) * dil
            valid = band & (kpos >= 0) & (kpos < slen)
            s = jnp.where(valid, s, NEG_INF)
            m = jnp.max(s, axis=-1, keepdims=True)
            p = jnp.exp(s - m)
            l = jnp.sum(p, axis=-1, keepdims=True)
            o = jnp.dot(p.astype(BF16), v.astype(BF16), preferred_element_type=F32) / l
            lse = m + jnp.log(l)
            ob[b, pl.ds(start, ATTN_CHUNK, stride=dil), :] = o
            lb[b, pl.ds(start, ATTN_CHUNK, stride=dil), :] = jnp.broadcast_to(lse, (ATTN_CHUNK, HEAD_DIM))
            return carry

        lax.fori_loop(0, tq // ATTN_CHUNK, body, 0, unroll=ATTN_UNROLL)

    l0, l1, l2 = lb[0], lb[1], lb[2]
    mx = jnp.maximum(jnp.maximum(l0, l1), l2)
    w0, w1, w2 = jnp.exp(l0 - mx), jnp.exp(l1 - mx), jnp.exp(l2 - mx)
    o = (w0 * ob[0] + w1 * ob[1] + w2 * ob[2]) / (w0 + w1 + w2)
    o_ref[...] = o.astype(o_ref.dtype)


def dilated_attention(qk, proj, meta, aw, tq):
    t = qk.shape[0]
    nh = aw // HEAD_DIM
    halo = ATTN_SIDE * max(d for _, d in DILATED_BRANCHES)
    assert tq % halo == 0
    hpb = tq // halo
    nhb = t // halo
    cur = lambda off: (lambda i, h, m: (i, off + h))
    prv = lambda off: (lambda i, h, m: (jnp.maximum(i * hpb - 1, 0), off + h))
    nxt = lambda off: (lambda i, h, m: (jnp.minimum((i + 1) * hpb, nhb - 1), off + h))
    grid_spec = pltpu.PrefetchScalarGridSpec(
        num_scalar_prefetch=1,
        grid=(t // tq, nh),
        in_specs=[
            pl.BlockSpec((tq, HEAD_DIM), cur(0)),
            pl.BlockSpec((tq, HEAD_DIM), cur(nh)),
            pl.BlockSpec((halo, HEAD_DIM), prv(nh)),
            pl.BlockSpec((halo, HEAD_DIM), nxt(nh)),
            pl.BlockSpec((tq, HEAD_DIM), cur(2 * nh)),
            pl.BlockSpec((halo, HEAD_DIM), prv(2 * nh)),
            pl.BlockSpec((halo, HEAD_DIM), nxt(2 * nh)),
        ],
        out_specs=pl.BlockSpec((tq, HEAD_DIM), lambda i, h, m: (i, h)),
        scratch_shapes=[
            pltpu.VMEM((tq, HEAD_DIM), F32),
            pltpu.VMEM((tq + 2 * halo, HEAD_DIM), F32),
            pltpu.VMEM((tq + 2 * halo, HEAD_DIM), F32),
            pltpu.VMEM((3, tq, HEAD_DIM), F32),
            pltpu.VMEM((3, tq, HEAD_DIM), F32),
        ],
    )
    return pl.pallas_call(
        functools.partial(_attn_kernel, tq=tq, halo=halo),
        out_shape=jax.ShapeDtypeStruct((t, aw), BF16),
        grid_spec=grid_spec,
        compiler_params=_params(("arbitrary", "arbitrary")),
        name="dilated_attn",
    )(meta, qk, qk, qk, qk, proj, proj, proj)


def _conv_kernel(meta_ref, xc_ref, xp_ref, xn_ref, w_ref, o_ref, xs, *, tb, hb, tc, n_qk_blocks):
    i = pl.program_id(0)
    j = pl.program_id(1)
    first = meta_ref[0, i] == 1
    last = meta_ref[1, i] == 1
    xs[hb:hb + tb] = xc_ref[...].astype(F32)
    xs[0:hb] = jnp.where(first, 0.0, xp_ref[...].astype(F32))
    xs[hb + tb:] = jnp.where(last, 0.0, xn_ref[...].astype(F32))
    cw = w_ref.shape[0]
    pad = cw // 2
    acc = w_ref[0:1, :] * xs[pl.ds(hb - pad, tb), :]
    for tap in range(1, cw):
        acc = acc + w_ref[tap:tap + 1, :] * xs[pl.ds(hb - pad + tap, tb), :]
    y = acc * (1.0 / (1.0 + jnp.exp(-acc)))
    is_qk = j < n_qk_blocks
    for g in range(tc // HEAD_DIM):
        sl = slice(g * HEAD_DIM, (g + 1) * HEAD_DIM)
        yg = y[:, sl]
        inv = lax.rsqrt(jnp.sum(yg * yg, axis=-1, keepdims=True) + EPS)
        o_ref[:, sl] = yg * jnp.where(is_qk, inv, 1.0)


def gdn_conv(proj, conv_w, meta, col0, gw):
    t = proj.shape[0]
    tb = _pick(t, (512, 256, 128))
    tc = _pick(np.gcd(col0, gw), (512, 256, 128))
    hb = BF16_SUBLANES
    c0 = col0 // tc
    return pl.pallas_call(
        functools.partial(_conv_kernel, tb=tb, hb=hb, tc=tc, n_qk_blocks=2 * gw // tc),
        out_shape=jax.ShapeDtypeStruct((t, 3 * gw), F32),
        grid_spec=pltpu.PrefetchScalarGridSpec(
            num_scalar_prefetch=1,
            grid=(t // tb, 3 * gw // tc),
            in_specs=[
                pl.BlockSpec((tb, tc), lambda i, j, m: (i, c0 + j)),
                pl.BlockSpec((hb, tc), lambda i, j, m: (jnp.maximum(i * (tb // hb) - 1, 0), c0 + j)),
                pl.BlockSpec((hb, tc), lambda i, j, m: (jnp.minimum((i + 1) * (tb // hb), t // hb - 1), c0 + j)),
                pl.BlockSpec((conv_w.shape[0], tc), lambda i, j, m: (0, j)),
            ],
            out_specs=pl.BlockSpec((tb, tc), lambda i, j, m: (i, j)),
            scratch_shapes=[pltpu.VMEM((tb + 2 * hb, tc), F32)],
        ),
        compiler_params=_params(("arbitrary", "arbitrary")),
        name="gdn_conv",
    )(meta, proj, proj, proj, conv_w)


GDN_ROWS = LANES
GDN_HEAD_GROUP = 8


def _gdn_gates(ab, alog_row, dtb_row):
    x = ab + dtb_row
    softplus = jnp.maximum(x, 0.0) + jnp.log(1.0 + jnp.exp(-jnp.abs(x)))
    return -jnp.exp(alog_row) * softplus, 1.0 / (1.0 + jnp.exp(-ab))


def _chunk_cumsum(g, reverse):
    C = GDN_CHUNK
    n = g.shape[0]
    pos = lax.broadcasted_iota(I32, g.shape, 0) & (C - 1)
    x = g
    s = 1
    while s < C:
        if reverse:
            x = x + jnp.where(pos < C - s, pltpu.roll(x, n - s, axis=0), 0.0)
        else:
            x = x + jnp.where(pos >= s, pltpu.roll(x, s, axis=0), 0.0)
        s *= 2
    return x


def _bdot(a, b):
    return jnp.dot(a.astype(BF16), b.astype(BF16), preferred_element_type=F32)


def _gdn_prep_kernel(q_ref, k_ref, v_ref, ab_ref, alog_ref, dtb_ref, *out_refs, n_heads):
    C = GDN_CHUNK
    n_chunks = q_ref.shape[0] // C
    g_all, beta_all = _gdn_gates(ab_ref[...], alog_ref[...], dtb_ref[...])
    gcs = []
    for reverse in (False, True):
        gc = _chunk_cumsum(g_all, reverse)
        gcs.append((gc, gc.T))

    ri = lax.broadcasted_iota(I32, (C, C), 0)
    ci = lax.broadcasted_iota(I32, (C, C), 1)
    eye = (ri == ci).astype(F32)
    masks = [(ri >= ci, ri > ci), (ri <= ci, ri < ci)]
    scale = HEAD_DIM ** -0.5
    nt = (((1,), (1,)), ((), ()))

    for c in range(n_chunks):
        rows = slice(c * C, (c + 1) * C)
        for h0 in range(0, n_heads, GDN_HEAD_GROUP):
            heads = range(h0, min(h0 + GDN_HEAD_GROUP, n_heads))
            items = [(h, d) for h in heads for d in (0, 1)]
            hs = {h: slice(h * HEAD_DIM, (h + 1) * HEAD_DIM) for h in heads}
            ks = {h: k_ref[rows, hs[h]] for h in heads}
            qs = {h: q_ref[rows, hs[h]] * scale for h in heads}
            a2 = {h: lax.dot_general(jnp.concatenate([ks[h], qs[h]], axis=0).astype(BF16),
                                     ks[h].astype(BF16), nt, preferred_element_type=F32)
                  for h in heads}
            beta, gcc, eg, tmat, lp = {}, {}, {}, {}, {}
            for (h, d) in items:
                gc, gct = gcs[d]
                b_col = 2 * n_heads * d
                g_col = b_col + n_heads
                incl, strict = masks[d]
                beta[h, d] = beta_all[rows, b_col + h:b_col + h + 1]
                gcc[h, d] = gc[rows, g_col + h:g_col + h + 1]
                gcr = gct[g_col + h:g_col + h + 1, rows]
                decay = jnp.where(incl, jnp.exp(jnp.where(incl, gcc[h, d] - gcr, 0.0)), 0.0)
                lmat = jnp.where(strict, a2[h][:C] * beta[h, d] * decay, 0.0)
                attn = jnp.where(incl, a2[h][C:] * decay, 0.0)
                out_refs[5 * d + 4][rows, h * C:(h + 1) * C] = attn.astype(BF16)
                tmat[h, d] = eye - lmat
                lp[h, d] = lmat
                eg[h, d] = jnp.exp(gcc[h, d])
            p = 2
            while p < C:
                for it in items:
                    lp[it] = _bdot(lp[it], lp[it])
                for it in items:
                    tmat[it] = tmat[it] + _bdot(tmat[it], lp[it])
                p *= 2
            for (h, d) in items:
                u_ref, w_ref, qg_ref, kdt_ref = out_refs[5 * d:5 * d + 4]
                kb = ks[h] * beta[h, d]
                vb = v_ref[rows, hs[h]] * beta[h, d]
                uw = _bdot(tmat[h, d], jnp.concatenate([vb, kb * eg[h, d]], axis=1))
                u_ref[rows, hs[h]] = uw[:, :HEAD_DIM].astype(BF16)
                w_ref[rows, hs[h]] = uw[:, HEAD_DIM:].astype(BF16)
                qg_ref[rows, hs[h]] = (qs[h] * eg[h, d]).astype(BF16)
                g_last_row = c * C if d else (c + 1) * C - 1
                g_col = 2 * n_heads * d + n_heads
                glast = gcs[d][0][g_last_row:g_last_row + 1, g_col + h:g_col + h + 1]
                kd = ks[h] * jnp.exp(glast - gcc[h, d])
                kdt_ref[c * HEAD_DIM:(c + 1) * HEAD_DIM, h * C:(h + 1) * C] = kd.T.astype(BF16)


def gdn_prep(qkv, ab, alog_row, dtb_row, gw):
    t = qkv.shape[0]
    nh = gw // HEAD_DIM
    rb = GDN_ROWS
    aw = nh * GDN_CHUNK
    assert aw % LANES == 0
    row_spec = lambda cb: pl.BlockSpec((rb, gw), lambda i: (i, cb))
    one_dir_shapes = [jax.ShapeDtypeStruct((t, gw), BF16)] * 3 + [
        jax.ShapeDtypeStruct((t // GDN_CHUNK * HEAD_DIM, aw), BF16), jax.ShapeDtypeStruct((t, aw), BF16)]
    one_dir_specs = [pl.BlockSpec((rb, gw), lambda i: (i, 0))] * 3 + [
        pl.BlockSpec((rb // GDN_CHUNK * HEAD_DIM, aw), lambda i: (i, 0)), pl.BlockSpec((rb, aw), lambda i: (i, 0))]
    return pl.pallas_call(
        functools.partial(_gdn_prep_kernel, n_heads=nh),
        out_shape=tuple(one_dir_shapes * 2),
        grid=(t // rb,),
        in_specs=[row_spec(0), row_spec(1), row_spec(2),
                  pl.BlockSpec((rb, LANES), lambda i: (i, 0)),
                  pl.BlockSpec((1, LANES), lambda i: (0, 0)),
                  pl.BlockSpec((1, LANES), lambda i: (0, 0))],
        out_specs=tuple(one_dir_specs * 2),
        compiler_params=_params(("arbitrary",)),
        name="gdn_prep",
    )(qkv, qkv, qkv, ab, alog_row, dtb_row)


def _gdn_scan_kernel(meta_ref, u_ref, w_ref, qg_ref, kdt_ref, at_ref, ab_ref, alog_ref, dtb_ref, *rest,
                     n_heads, reverse, finalize):
    if finalize:
        ofwd_ref, z_ref, nw_ref, o_ref, s_ref = rest
    else:
        o_ref, s_ref = rest
    C = GDN_CHUNK
    n_chunks = u_ref.shape[0] // C
    i = pl.program_id(0)

    @pl.when(meta_ref[0, i] == 1)
    def _():
        s_ref[...] = jnp.zeros_like(s_ref)

    g_all, _ = _gdn_gates(ab_ref[...], alog_ref[...], dtb_ref[...])
    g_col = (3 if reverse else 1) * n_heads
    heads = range(n_heads)
    hs = [slice(h * HEAD_DIM, (h + 1) * HEAD_DIM) for h in heads]
    order = range(n_chunks - 1, -1, -1) if reverse else range(n_chunks)
    for c in order:
        rows = slice(c * C, (c + 1) * C)
        chunk_decay = jnp.exp(jnp.sum(g_all[rows], axis=0, keepdims=True))
        wq = [jnp.dot(jnp.concatenate([w_ref[rows, hs[h]], qg_ref[rows, hs[h]]], axis=0),
                      s_ref[h].astype(BF16), preferred_element_type=F32) for h in heads]
        vn = [(u_ref[rows, hs[h]].astype(F32) - wq[h][:C]).astype(BF16) for h in heads]
        outs = [wq[h][C:] + jnp.dot(at_ref[rows, h * C:(h + 1) * C], vn[h], preferred_element_type=F32)
                for h in heads]
        for h in heads:
            s_ref[h] = (s_ref[h] * chunk_decay[:, g_col + h:g_col + h + 1]
                        + jnp.dot(kdt_ref[c * HEAD_DIM:(c + 1) * HEAD_DIM, h * C:(h + 1) * C], vn[h],
                                  preferred_element_type=F32))
        for h in heads:
            o = outs[h]
            if finalize:
                o = o + ofwd_ref[rows, hs[h]]
                o = o * lax.rsqrt(jnp.mean(o * o, axis=-1, keepdims=True) + EPS) * nw_ref[...]
                z = z_ref[rows, hs[h]].astype(F32)
                o = o * (z * (1.0 / (1.0 + jnp.exp(-z))))
            o_ref[rows, hs[h]] = o.astype(o_ref.dtype)


def gdn_scan(prep, ab, alog_row, dtb_row, meta, gw, reverse, ofwd=None, proj=None, z_col=None, norm_w=None):
    u, w, qg, kdt, attn = prep
    t = u.shape[0]
    nh = gw // HEAD_DIM
    rb = GDN_ROWS
    nb = t // rb
    aw = attn.shape[1]
    finalize = ofwd is not None
    idx = (lambda i, m: (nb - 1 - i)) if reverse else (lambda i, m: i)
    wide = pl.BlockSpec((rb, gw), lambda i, m: (idx(i, m), 0))
    in_specs = [wide, wide, wide,
                pl.BlockSpec((rb // GDN_CHUNK * HEAD_DIM, aw), lambda i, m: (idx(i, m), 0)),
                pl.BlockSpec((rb, aw), lambda i, m: (idx(i, m), 0)),
                pl.BlockSpec((rb, LANES), lambda i, m: (idx(i, m), 0)),
                pl.BlockSpec((1, LANES), lambda i, m: (0, 0)),
                pl.BlockSpec((1, LANES), lambda i, m: (0, 0))]
    args = [meta, u, w, qg, kdt, attn, ab, alog_row, dtb_row]
    if finalize:
        assert z_col % gw == 0
        in_specs += [wide,
                     pl.BlockSpec((rb, gw), lambda i, m: (idx(i, m), z_col // gw)),
                     pl.BlockSpec((1, HEAD_DIM), lambda i, m: (0, 0))]
        args += [ofwd, proj, norm_w]
    return pl.pallas_call(
        functools.partial(_gdn_scan_kernel, n_heads=nh, reverse=reverse, finalize=finalize),
        out_shape=jax.ShapeDtypeStruct((t, gw), BF16 if finalize else F32),
        grid_spec=pltpu.PrefetchScalarGridSpec(
            num_scalar_prefetch=1,
            grid=(nb,),
            in_specs=in_specs,
            out_specs=wide,
            scratch_shapes=[pltpu.VMEM((nh, HEAD_DIM, HEAD_DIM), F32)],
        ),
        compiler_params=_params(("arbitrary",)),
        name="gdn_scan_bwd" if reverse else "gdn_scan_fwd",
    )(*args)


def _outproj_kernel(oa_ref, og_ref, wa_ref, wg_ref, xp_ref, xs_ref, o_ref, *, nbp):
    i = pl.program_id(0)
    mixed = (jnp.dot(oa_ref[...], wa_ref[...], preferred_element_type=F32)
             + jnp.dot(og_ref[...], wg_ref[...], preferred_element_type=F32))

    @pl.when(i < nbp)
    def _():
        o_ref[...] = xp_ref[...] + mixed

    @pl.when(i >= nbp)
    def _():
        o_ref[...] = xs_ref[...] + mixed


def out_projection(oa, og, w_out, xp, xs):
    t, aw = oa.shape
    gw = og.shape[1]
    d = w_out.shape[1]
    tp = xp.shape[0]
    tm = _pick(np.gcd(tp, xs.shape[0]), (1024, 512, 256, 128))
    tn = _pick(d, (512, 256, 128))
    nbp = tp // tm
    assert aw % 16 == 0
    return pl.pallas_call(
        functools.partial(_outproj_kernel, nbp=nbp),
        out_shape=jax.ShapeDtypeStruct((t, d), F32),
        grid=(t // tm, d // tn),
        in_specs=[
            pl.BlockSpec((tm, aw), lambda i, j: (i, 0)),
            pl.BlockSpec((tm, gw), lambda i, j: (i, 0)),
            pl.BlockSpec((aw, tn), lambda i, j: (0, j)),
            pl.BlockSpec((gw, tn), lambda i, j: (aw // gw, j)),
            pl.BlockSpec((tm, tn), lambda i, j: (jnp.minimum(i, nbp - 1), jnp.where(i < nbp, j, 0))),
            pl.BlockSpec((tm, tn), lambda i, j: (jnp.maximum(i - nbp, 0), jnp.where(i >= nbp, j, 0))),
        ],
        out_specs=pl.BlockSpec((tm, tn), lambda i, j: (i, j)),
        compiler_params=_params(("arbitrary", "arbitrary")),
        name="out_proj",
    )(oa, og, w_out, w_out, xp, xs)


def _router_kernel(x_ref, g_ref, wr_ref, br_ref, h_ref, idx_ref, gate_ref):
    h = _rms(x_ref[...], g_ref[...])
    h_ref[...] = h
    logits = jnp.dot(h, wr_ref[...], precision=lax.Precision.HIGHEST,
                     preferred_element_type=F32) + br_ref[...]
    col = lax.broadcasted_iota(I32, logits.shape, 1).astype(F32)
    vals = logits
    tops, idxs = [], []
    for _ in range(TOP_K):
        m = jnp.max(vals, axis=-1, keepdims=True)
        idx = jnp.min(jnp.where(vals == m, col, float(LANES)), axis=-1, keepdims=True)
        tops.append(m)
        idxs.append(idx)
        vals = jnp.where(col == idx, -jnp.inf, vals)
    es = [jnp.exp(v - tops[0]) for v in tops]
    den = es[0]
    for e in es[1:]:
        den = den + e
    idx_out = jnp.zeros(logits.shape, F32)
    gate_out = jnp.zeros(logits.shape, F32)
    for k in range(TOP_K):
        idx_out = jnp.where(col == float(k), idxs[k], idx_out)
        gate_out = jnp.where(col == float(k), es[k] / den, gate_out)
    idx_ref[...] = idx_out.astype(I32)
    gate_ref[...] = gate_out


def ffn_norm_router(x1, g, w_router, b_router):
    t, d = x1.shape
    e = w_router.shape[1]
    assert TOP_K <= e <= LANES
    wr = jnp.zeros((d, LANES), F32).at[:, :e].set(w_router)
    br = jnp.full((1, LANES), NEG_INF, F32).at[0, :e].set(b_router)
    tr = _pick(t, (256, 128))
    return pl.pallas_call(
        _router_kernel,
        out_shape=(jax.ShapeDtypeStruct((t, d), F32),
                   jax.ShapeDtypeStruct((t, LANES), I32),
                   jax.ShapeDtypeStruct((t, LANES), F32)),
        grid=(t // tr,),
        in_specs=[
            pl.BlockSpec((tr, d), lambda i: (i, 0)),
            pl.BlockSpec((1, d), lambda i: (0, 0)),
            pl.BlockSpec((d, LANES), lambda i: (0, 0)),
            pl.BlockSpec((1, LANES), lambda i: (0, 0)),
        ],
        out_specs=(pl.BlockSpec((tr, d), lambda i: (i, 0)),
                   pl.BlockSpec((tr, LANES), lambda i: (i, 0)),
                   pl.BlockSpec((tr, LANES), lambda i: (i, 0))),
        compiler_params=_params(("arbitrary",)),
        name="ffn_norm_router",
    )(x1, g.reshape(1, d), wr, br)


def _gather_rows(tok_ref, src_hbm, dst, sem, n_rows, row0=0):
    def body(j, carry):
        r = row0 + j
        tok = tok_ref[0, 0, r]
        pltpu.make_async_copy(src_hbm.at[pl.ds(tok, 1)], dst.at[pl.ds(r, 1)], sem).start()
        return carry
    lax.fori_loop(0, n_rows, body, 0, unroll=GATHER_UNROLL)


def _wait_rows(src_hbm, dst, sem):
    pltpu.make_async_copy(src_hbm.at[pl.ds(0, dst.shape[0])], dst, sem).wait()


def _moe_up_kernel(be_ref, nv_ref, tokc_ref, tokn_ref, x_hbm, wg_ref, bg_ref, wu_ref, bu_ref, h_ref,
                   xg, xb, sem, *, tm, nt):
    m = pl.program_id(0)
    n = pl.program_id(1)
    nvalid = nv_ref[0]
    live = m < nvalid
    rows_per_step = tm // nt

    @pl.when(live & (n == 0))
    def _():
        @pl.when(m == 0)
        def _():
            _gather_rows(tokc_ref, x_hbm, xg, sem, tm)
        _wait_rows(x_hbm, xg, sem)
        xb[...] = xg[...].astype(BF16)

    @pl.when(m + 1 < nvalid)
    def _():
        _gather_rows(tokn_ref, x_hbm, xg, sem, rows_per_step, row0=n * rows_per_step)

    @pl.when(live)
    def _():
        x = xb[...]
        gate = jnp.dot(x, wg_ref[...].astype(BF16), preferred_element_type=F32) + bg_ref[...]
        up = jnp.dot(x, wu_ref[...].astype(BF16), preferred_element_type=F32) + bu_ref[...]
        gate = jnp.minimum(gate, SWIGLU_LIMIT)
        up = jnp.clip(up, -SWIGLU_LIMIT, SWIGLU_LIMIT)
        glu = gate * (1.0 / (1.0 + jnp.exp(-gate * SWIGLU_ALPHA)))
        h_ref[...] = ((up + 1.0) * glu).astype(h_ref.dtype)

    @pl.when(jnp.logical_not(live))
    def _():
        h_ref[...] = jnp.zeros_like(h_ref)


def moe_up(h2, row_tok3, block_e, nvalid, w_gate, b_gate, w_up, b_up, tm):
    n_blocks = row_tok3.shape[0]
    e, d, f = w_gate.shape
    tn = _pick(f, (256, 128))
    nt = f // tn
    wmap = lambda m, n, be, nv: (be[m], 0, jnp.where(m < nv[0], n, nt - 1))
    return pl.pallas_call(
        functools.partial(_moe_up_kernel, tm=tm, nt=nt),
        out_shape=jax.ShapeDtypeStruct((n_blocks * tm, f), BF16),
        grid_spec=pltpu.PrefetchScalarGridSpec(
            num_scalar_prefetch=2,
            grid=(n_blocks, nt),
            in_specs=[
                pl.BlockSpec((1, 1, tm), lambda m, n, be, nv: (m, 0, 0), memory_space=pltpu.SMEM),
                pl.BlockSpec((1, 1, tm), lambda m, n, be, nv: (jnp.minimum(m + 1, n_blocks - 1), 0, 0),
                             memory_space=pltpu.SMEM),
                pl.BlockSpec(memory_space=pl.ANY),
                pl.BlockSpec((None, d, tn), wmap),
                pl.BlockSpec((None, 1, tn), wmap),
                pl.BlockSpec((None, d, tn), wmap),
                pl.BlockSpec((None, 1, tn), wmap),
            ],
            out_specs=pl.BlockSpec((tm, tn), lambda m, n, be, nv: (m, n)),
            scratch_shapes=[pltpu.VMEM((tm, d), F32), pltpu.VMEM((tm, d), BF16),
                            pltpu.SemaphoreType.DMA(())],
        ),
        compiler_params=_params(("arbitrary", "arbitrary")),
        name="moe_up",
    )(block_e, nvalid, row_tok3, row_tok3, h2, w_gate, b_gate.reshape(e, 1, f), w_up, b_up.reshape(e, 1, f))


def _moe_down_kernel(be_ref, nv_ref, h_ref, wd_ref, bd_ref, o_ref):
    m = pl.program_id(0)
    live = m < nv_ref[0]

    @pl.when(live)
    def _():
        o_ref[...] = jnp.dot(h_ref[...], wd_ref[...].astype(BF16), preferred_element_type=F32) + bd_ref[...]

    @pl.when(jnp.logical_not(live))
    def _():
        o_ref[...] = jnp.zeros_like(o_ref)


def moe_down(hmid, block_e, nvalid, w_down, b_down, tm):
    n_rows, f = hmid.shape
    e, _, d = w_down.shape
    n_blocks = n_rows // tm
    tn = _pick(d, (512, 256, 128))
    nt = d // tn
    wmap = lambda m, n, be, nv: (be[m], 0, jnp.where(m < nv[0], n, nt - 1))
    return pl.pallas_call(
        _moe_down_kernel,
        out_shape=jax.ShapeDtypeStruct((n_rows, d), F32),
        grid_spec=pltpu.PrefetchScalarGridSpec(
            num_scalar_prefetch=2,
            grid=(n_blocks, nt),
            in_specs=[
                pl.BlockSpec((tm, f), lambda m, n, be, nv: (m, 0)),
                pl.BlockSpec((None, f, tn), wmap),
                pl.BlockSpec((None, 1, tn), wmap),
            ],
            out_specs=pl.BlockSpec((tm, tn), lambda m, n, be, nv: (m, n)),
        ),
        compiler_params=_params(("arbitrary", "arbitrary")),
        name="moe_down",
    )(block_e, nvalid, hmid, w_down, b_down.reshape(e, 1, d))


def _combine_kernel(destc_ref, destn_ref, rows_hbm, x_ref, gate_ref, g_ref, o_ref, buf, sem, *, tr, nb):
    i = pl.program_id(0)
    slot = i % 2
    n_rows = TOP_K * tr

    @pl.when(i == 0)
    def _():
        _gather_rows(destc_ref, rows_hbm, buf.at[0], sem.at[0], n_rows)

    @pl.when(i + 1 < nb)
    def _():
        _gather_rows(destn_ref, rows_hbm, buf.at[1 - slot], sem.at[1 - slot], n_rows)

    _wait_rows(rows_hbm, buf.at[slot], sem.at[slot])
    y = x_ref[...]
    gates = gate_ref[...]
    for k in range(TOP_K):
        y = y + gates[:, k:k + 1] * buf[slot, pl.ds(k * tr, tr), :]
    o_ref[...] = _rms(y, g_ref[...])


def moe_combine_norm(x1, rows, dest3, gates, g, row0, n_tok, tr):
    d = x1.shape[1]
    nb = n_tok // tr
    b0 = row0 // tr
    nbt = dest3.shape[0]
    return pl.pallas_call(
        functools.partial(_combine_kernel, tr=tr, nb=nb),
        out_shape=jax.ShapeDtypeStruct((n_tok, d), F32),
        grid=(nb,),
        in_specs=[
            pl.BlockSpec((1, 1, TOP_K * tr), lambda i: (b0 + i, 0, 0), memory_space=pltpu.SMEM),
            pl.BlockSpec((1, 1, TOP_K * tr), lambda i: (jnp.minimum(b0 + i + 1, nbt - 1), 0, 0),
                         memory_space=pltpu.SMEM),
            pl.BlockSpec(memory_space=pl.ANY),
            pl.BlockSpec((tr, d), lambda i: (b0 + i, 0)),
            pl.BlockSpec((tr, LANES), lambda i: (b0 + i, 0)),
            pl.BlockSpec((1, d), lambda i: (0, 0)),
        ],
        out_specs=pl.BlockSpec((tr, d), lambda i: (i, 0)),
        scratch_shapes=[pltpu.VMEM((2, TOP_K * tr, d), F32), pltpu.SemaphoreType.DMA((2,))],
        compiler_params=_params(("arbitrary",)),
        name="moe_combine_norm",
    )(dest3, dest3, rows, x1, gates, g.reshape(1, d))


def _route(top_idx, n_experts, tm):
    t = top_idx.shape[0]
    n_assign = t * TOP_K
    flat_e = top_idx.reshape(-1)
    onehot = (flat_e[:, None] == jnp.arange(n_experts, dtype=I32)[None, :]).astype(I32)
    csum = jnp.cumsum(onehot, axis=0)
    counts = csum[-1]
    rank = jnp.take_along_axis(csum, flat_e[:, None], axis=1)[:, 0] - 1
    padded = ((counts + tm - 1) // tm) * tm
    pend = jnp.cumsum(padded)
    pstart = pend - padded
    dest = pstart[flat_e] + rank
    n_blocks = -(-n_assign // tm) + n_experts
    flat_tok = jnp.arange(n_assign, dtype=I32) // TOP_K
    row_tok = jnp.zeros((n_blocks * tm,), I32).at[dest].set(flat_tok, unique_indices=True)
    block_e = jnp.minimum(
        jnp.searchsorted(pend, jnp.arange(n_blocks, dtype=I32) * tm, side='right'), n_experts - 1).astype(I32)
    nvalid = (pend[-1] // tm).astype(I32).reshape(1)
    return row_tok, dest.reshape(t, TOP_K).astype(I32), block_e, nvalid


def _seq_table(seqs, blk):
    pos0, slen, first, last = [], [], [], []
    for (b, s) in seqs:
        assert s % blk == 0
        nb = s // blk
        for _ in range(b):
            for j in range(nb):
                pos0.append(j * blk)
                slen.append(s)
                first.append(int(j == 0))
                last.append(int(j == nb - 1))
    return np.array(pos0, np.int32), np.array(slen, np.int32), np.array(first, np.int32), np.array(last, np.int32)


def _rope_tables(seqs):
    half = HEAD_DIM // 2
    inv = ROPE_THETA ** (-jnp.arange(0, HEAD_DIM, 2, dtype=F32) / HEAD_DIM)
    cos_l, sin_l = [], []
    for (b, s) in seqs:
        ang = jnp.arange(s, dtype=F32)[:, None] * inv[None, :]
        c = jnp.cos(ang)
        sn = jnp.sin(ang)
        cos_l.append(jnp.tile(jnp.concatenate([c, c], axis=-1), (b, 1)))
        sin_l.append(jnp.tile(jnp.concatenate([-sn, sn], axis=-1), (b, 1)))
    del half
    return jnp.concatenate(cos_l, axis=0), jnp.concatenate(sin_l, axis=0)


def _layer(xp, xs, seqs, norm_mix, w_in, conv_w, a_log_fwd, dt_bias_fwd, a_log_bwd, dt_bias_bwd,
           gdn_norm, w_out, norm_ffn, w_router, b_router, w_gate, b_gate, w_up, b_up, w_down, b_down,
           norm_final):
    moe_tm, comb_tr, attn_tq = MOE_ROW_BLOCK, COMBINE_TOKEN_BLOCK, ATTN_Q_BLOCK
    d = xp.shape[1]
    nhg = a_log_fwd.shape[0]
    gw = nhg * HEAD_DIM
    aw = w_out.shape[0] - gw
    n_main = 3 * aw + 4 * gw
    assert w_in.shape[1] == n_main + 4 * nhg and 4 * nhg <= LANES
    t = xp.shape[0] + xs.shape[0]

    h = norm_two_streams(xp, xs, norm_mix)
    w_main = w_in[:, :n_main].astype(BF16)
    w_tail = jnp.zeros((d, LANES), BF16).at[:, :4 * nhg].set(w_in[:, n_main:].astype(BF16))
    proj, ab = in_projection(h, w_main, w_tail)

    cos, sin = _rope_tables(seqs)
    qk = rope_qk(proj, cos, sin, aw)
    pos0, slen, _, _ = _seq_table(seqs, attn_tq)
    o_attn = dilated_attention(qk, proj, jnp.asarray(np.stack([pos0, slen])), aw, attn_tq)

    conv_tb = _pick(t, (512, 256, 128))
    _, _, cfirst, clast = _seq_table(seqs, conv_tb)
    qkv = gdn_conv(proj, conv_w, jnp.asarray(np.stack([cfirst, clast])), 3 * aw, gw)
    _, _, gfirst, glast = _seq_table(seqs, LANES)
    zrow = jnp.zeros((nhg,), F32)
    alog_row = jnp.zeros((1, LANES), F32).at[0, :4 * nhg].set(
        jnp.concatenate([zrow, a_log_fwd, zrow, a_log_bwd]))
    dtb_row = jnp.zeros((1, LANES), F32).at[0, :4 * nhg].set(
        jnp.concatenate([zrow, dt_bias_fwd, zrow, dt_bias_bwd]))
    prep = gdn_prep(qkv, ab, alog_row, dtb_row, gw)
    o_fwd = gdn_scan(prep[:5], ab, alog_row, dtb_row, jnp.asarray(gfirst[None, :]), gw, reverse=False)
    o_gdn = gdn_scan(prep[5:], ab, alog_row, dtb_row, jnp.asarray(glast[::-1][None, :].copy()), gw, reverse=True,
                     ofwd=o_fwd, proj=proj, z_col=3 * aw + 3 * gw, norm_w=gdn_norm.reshape(1, HEAD_DIM))

    x1 = out_projection(o_attn, o_gdn, w_out.astype(BF16), xp, xs)

    h2, idx_pad, gate_pad = ffn_norm_router(x1, norm_ffn, w_router, b_router)
    n_experts = w_router.shape[1]
    row_tok, dest, block_e, nvalid = _route(idx_pad[:, :TOP_K], n_experts, moe_tm)
    n_blocks = block_e.shape[0]
    hmid = moe_up(h2, row_tok.reshape(n_blocks, 1, moe_tm), block_e, nvalid, w_gate, b_gate, w_up, b_up, moe_tm)
    rows = moe_down(hmid, block_e, nvalid, w_down, b_down, moe_tm)
    dest3 = dest.reshape(t // comb_tr, 1, comb_tr, TOP_K).transpose(0, 1, 3, 2).reshape(t // comb_tr, 1, TOP_K * comb_tr)
    tp = xp.shape[0]
    yp = moe_combine_norm(x1, rows, dest3, gate_pad, norm_final, 0, tp, comb_tr)
    ys = moe_combine_norm(x1, rows, dest3, gate_pad, norm_final, tp, xs.shape[0], comb_tr)
    return yp, ys


def kernel(x_prompt, x_sample, norm_mix, w_in, conv_w, a_log_fwd, dt_bias_fwd, a_log_bwd, dt_bias_bwd,
           gdn_norm, w_out, norm_ffn, w_router, b_router, w_gate, b_gate, w_up, b_up, w_down, b_down,
           norm_final):
    assert norm_mix.shape[0] == 1, "single-layer encoder"
    bp, sp, d = x_prompt.shape
    bs, ss, _ = x_sample.shape
    seqs = ((bp, sp), (bs, ss))
    layer0 = lambda a: a.reshape(a.shape[1:])
    yp, ys = _layer(
        x_prompt.reshape(bp * sp, d), x_sample.reshape(bs * ss, d), seqs,
        *[layer0(a) for a in (norm_mix, w_in, conv_w, a_log_fwd, dt_bias_fwd, a_log_bwd, dt_bias_bwd,
                              gdn_norm, w_out, norm_ffn, w_router, b_router, w_gate, b_gate, w_up,
                              b_up, w_down, b_down)],
        norm_final)
    return yp.reshape(bp, sp, d), ys.reshape(bs, ss, d)
```

```python
import functools

import numpy as np
import jax
import jax.numpy as jnp
from jax import lax
from jax.experimental import pallas as pl
from jax.experimental.pallas import tpu as pltpu

F32 = jnp.float32
BF16 = jnp.bfloat16
I32 = jnp.int32

HEAD_DIM = 128
DILATED_BRANCHES = ((128, 1), (512, 4), (2048, 16))
ATTN_SIDE = 64
ATTN_CHUNK = 2 * ATTN_SIDE
ROPE_THETA = 10000.0
GDN_CHUNK = 64
TOP_K = 4
SWIGLU_LIMIT = 7.0
SWIGLU_ALPHA = 1.702
EPS = 1e-6
NEG_INF = -1e30
LANES = 128
BF16_SUBLANES = 16
VMEM_LIMIT = 56 * 1024 * 1024
MOE_ROW_BLOCK = 1024
COMBINE_TOKEN_BLOCK = 128
ATTN_Q_BLOCK = 2048
GATHER_UNROLL = 8
ATTN_UNROLL = 8


def _pick(n, cands):
    for c in cands:
        if n % c == 0:
            return c
    raise ValueError(f"no tile in {cands} divides {n}")


def _params(sem, vmem=VMEM_LIMIT):
    return pltpu.CompilerParams(dimension_semantics=sem, vmem_limit_bytes=vmem)


def _rms(x, g):
    ms = jnp.mean(x * x, axis=-1, keepdims=True)
    return x * lax.rsqrt(ms + EPS) * g


def _norm2_kernel(xp_ref, xs_ref, g_ref, o_ref, *, nbp):
    i = pl.program_id(0)

    @pl.when(i < nbp)
    def _():
        o_ref[...] = _rms(xp_ref[...], g_ref[...]).astype(o_ref.dtype)

    @pl.when(i >= nbp)
    def _():
        o_ref[...] = _rms(xs_ref[...], g_ref[...]).astype(o_ref.dtype)


def norm_two_streams(xp, xs, g):
    tp, d = xp.shape
    ts = xs.shape[0]
    tr = _pick(np.gcd(tp, ts), (256, 128, 64, 32, 16))
    nbp = tp // tr
    nb = nbp + ts // tr
    return pl.pallas_call(
        functools.partial(_norm2_kernel, nbp=nbp),
        out_shape=jax.ShapeDtypeStruct((tp + ts, d), BF16),
        grid=(nb,),
        in_specs=[
            pl.BlockSpec((tr, d), lambda i: (jnp.minimum(i, nbp - 1), 0)),
            pl.BlockSpec((tr, d), lambda i: (jnp.maximum(i - nbp, 0), 0)),
            pl.BlockSpec((1, d), lambda i: (0, 0)),
        ],
        out_specs=pl.BlockSpec((tr, d), lambda i: (i, 0)),
        compiler_params=_params(("arbitrary",)),
        name="norm_mix",
    )(xp, xs, g.reshape(1, d))


def _inproj_kernel(h_ref, w_ref, wt_ref, o_ref, ab_ref):
    j = pl.program_id(1)
    o_ref[...] = jnp.dot(h_ref[...], w_ref[...], preferred_element_type=F32).astype(o_ref.dtype)

    @pl.when(j == 0)
    def _():
        ab_ref[...] = jnp.dot(h_ref[...], wt_ref[...], preferred_element_type=F32)


def in_projection(h, w_main, w_tail):
    t, d = h.shape
    n = w_main.shape[1]
    tm = _pick(t, (1024, 512, 256, 128))
    tn = _pick(n, (512, 256, 128))
    return pl.pallas_call(
        _inproj_kernel,
        out_shape=(jax.ShapeDtypeStruct((t, n), BF16), jax.ShapeDtypeStruct((t, LANES), F32)),
        grid=(t // tm, n // tn),
        in_specs=[
            pl.BlockSpec((tm, d), lambda i, j: (i, 0)),
            pl.BlockSpec((d, tn), lambda i, j: (0, j)),
            pl.BlockSpec((d, LANES), lambda i, j: (0, 0)),
        ],
        out_specs=(
            pl.BlockSpec((tm, tn), lambda i, j: (i, j)),
            pl.BlockSpec((tm, LANES), lambda i, j: (i, 0)),
        ),
        compiler_params=_params(("arbitrary", "arbitrary")),
        name="in_proj",
    )(h, w_main, w_tail)


def _rope_kernel(x_ref, cos_ref, sin_ref, o_ref, *, n_heads):
    cos = cos_ref[...]
    sin = sin_ref[...]
    for h in range(n_heads):
        sl = slice(h * HEAD_DIM, (h + 1) * HEAD_DIM)
        x = x_ref[:, sl].astype(F32)
        rot = pltpu.roll(x, HEAD_DIM // 2, axis=1)
        o_ref[:, sl] = (x * cos + rot * sin).astype(o_ref.dtype)


def rope_qk(proj, cos, sin, aw):
    t = proj.shape[0]
    tr = _pick(t, (1024, 512, 256))
    return pl.pallas_call(
        functools.partial(_rope_kernel, n_heads=aw // HEAD_DIM),
        out_shape=jax.ShapeDtypeStruct((t, 2 * aw), BF16),
        grid=(t // tr, 2),
        in_specs=[
            pl.BlockSpec((tr, aw), lambda i, j: (i, j)),
            pl.BlockSpec((tr, HEAD_DIM), lambda i, j: (i, 0)),
            pl.BlockSpec((tr, HEAD_DIM), lambda i, j: (i, 0)),
        ],
        out_specs=pl.BlockSpec((tr, aw), lambda i, j: (i, j)),
        compiler_params=_params(("arbitrary", "arbitrary")),
        name="rope_qk",
    )(proj, cos, sin)


def _attn_kernel(meta_ref, q_ref, kc_ref, kp_ref, kn_ref, vc_ref, vp_ref, vn_ref, o_ref,
                 qf, kw, vw, ob, lb, *, tq, halo):
    i = pl.program_id(0)
    pos0 = meta_ref[0, i]
    slen = meta_ref[1, i]
    qf[...] = q_ref[...].astype(F32)
    kw[0:halo] = kp_ref[...].astype(F32)
    kw[halo:halo + tq] = kc_ref[...].astype(F32)
    kw[halo + tq:] = kn_ref[...].astype(F32)
    vw[0:halo] = vp_ref[...].astype(F32)
    vw[halo:halo + tq] = vc_ref[...].astype(F32)
    vw[halo + tq:] = vn_ref[...].astype(F32)

    scale = HEAD_DIM ** -0.5
    nk = 2 * ATTN_CHUNK
    a_idx = lax.broadcasted_iota(I32, (ATTN_CHUNK, nk), 0)
    j_idx = lax.broadcasted_iota(I32, (ATTN_CHUNK, nk), 1)
    rel = j_idx - a_idx
    band = (rel >= 0) & (rel <= 2 * ATTN_SIDE)

    for b, (_, dil) in enumerate(DILATED_BRANCHES):
        def body(c, carry, b=b, dil=dil):
            r = c % dil
            start = r + (c // dil) * (dil * ATTN_CHUNK)
            q = qf[pl.ds(start, ATTN_CHUNK, stride=dil), :]
            kstart = halo + start - ATTN_SIDE * dil
            k = kw[pl.ds(kstart, nk, stride=dil), :]
            v = vw[pl.ds(kstart, nk, stride=dil), :]
            s = lax.dot_general(q.astype(BF16), k.astype(BF16), (((1,), (1,)), ((), ())),
                                preferred_element_type=F32) * scale
            kpos = (pos0 + start) + (j_idx - ATTN_SIDE) * dil
            valid = band & (kpos >= 0) & (kpos < slen)
            s = jnp.where(valid, s, NEG_INF)
            m = jnp.max(s, axis=-1, keepdims=True)
            p = jnp.exp(s - m)
            l = jnp.sum(p, axis=-1, keepdims=True)
            o = jnp.dot(p.astype(BF16), v.astype(BF16), preferred_element_type=F32) / l
            lse = m + jnp.log(l)
            ob[b, pl.ds(start, ATTN_CHUNK, stride=dil), :] = o
            lb[b, pl.ds(start, ATTN_CHUNK, stride=dil), :] = jnp.broadcast_to(lse, (ATTN_CHUNK, HEAD_DIM))
            return carry

        lax.fori_loop(0, tq // ATTN_CHUNK, body, 0, unroll=ATTN_UNROLL)

    l0, l1, l2 = lb[0], lb[1], lb[2]
    mx = jnp.maximum(jnp.maximum(l0, l1), l2)
    w0, w1, w2 = jnp.exp(l0 - mx), jnp.exp(l1 - mx), jnp.exp(l2 - mx)
    o = (w0 * ob[0] + w1 * ob[1] + w2 * ob[2]) / (w0 + w1 + w2)
    o_ref[...] = o.astype(o_ref.dtype)


def dilated_attention(qk, proj, meta, aw, tq):
    t = qk.shape[0]
    nh = aw // HEAD_DIM
    halo = ATTN_SIDE * max(d for _, d in DILATED_BRANCHES)
    assert tq % halo == 0
    hpb = tq // halo
    nhb = t // halo
    cur = lambda off: (lambda i, h, m: (i, off + h))
    prv = lambda off: (lambda i, h, m: (jnp.maximum(i * hpb - 1, 0), off + h))
    nxt = lambda off: (lambda i, h, m: (jnp.minimum((i + 1) * hpb, nhb - 1), off + h))
    grid_spec = pltpu.PrefetchScalarGridSpec(
        num_scalar_prefetch=1,
        grid=(t // tq, nh),
        in_specs=[
            pl.BlockSpec((tq, HEAD_DIM), cur(0)),
            pl.BlockSpec((tq, HEAD_DIM), cur(nh)),
            pl.BlockSpec((halo, HEAD_DIM), prv(nh)),
            pl.BlockSpec((halo, HEAD_DIM), nxt(nh)),
            pl.BlockSpec((tq, HEAD_DIM), cur(2 * nh)),
            pl.BlockSpec((halo, HEAD_DIM), prv(2 * nh)),
            pl.BlockSpec((halo, HEAD_DIM), nxt(2 * nh)),
        ],
        out_specs=pl.BlockSpec((tq, HEAD_DIM), lambda i, h, m: (i, h)),
        scratch_shapes=[
            pltpu.VMEM((tq, HEAD_DIM), F32),
            pltpu.VMEM((tq + 2 * halo, HEAD_DIM), F32),
            pltpu.VMEM((tq + 2 * halo, HEAD_DIM), F32),
            pltpu.VMEM((3, tq, HEAD_DIM), F32),
            pltpu.VMEM((3, tq, HEAD_DIM), F32),
        ],
    )
    return pl.pallas_call(
        functools.partial(_attn_kernel, tq=tq, halo=halo),
        out_shape=jax.ShapeDtypeStruct((t, aw), BF16),
        grid_spec=grid_spec,
        compiler_params=_params(("arbitrary", "arbitrary")),
        name="dilated_attn",
    )(meta, qk, qk, qk, qk, proj, proj, proj)


def _conv_kernel(meta_ref, xc_ref, xp_ref, xn_ref, w_ref, o_ref, xs, *, tb, hb, tc, n_qk_blocks):
    i = pl.program_id(0)
    j = pl.program_id(1)
    first = meta_ref[0, i] == 1
    last = meta_ref[1, i] == 1
    xs[hb:hb + tb] = xc_ref[...].astype(F32)
    xs[0:hb] = jnp.where(first, 0.0, xp_ref[...].astype(F32))
    xs[hb + tb:] = jnp.where(last, 0.0, xn_ref[...].astype(F32))
    cw = w_ref.shape[0]
    pad = cw // 2
    acc = w_ref[0:1, :] * xs[pl.ds(hb - pad, tb), :]
    for tap in range(1, cw):
        acc = acc + w_ref[tap:tap + 1, :] * xs[pl.ds(hb - pad + tap, tb), :]
    y = acc * (1.0 / (1.0 + jnp.exp(-acc)))
    is_qk = j < n_qk_blocks
    for g in range(tc // HEAD_DIM):
        sl = slice(g * HEAD_DIM, (g + 1) * HEAD_DIM)
        yg = y[:, sl]
        inv = lax.rsqrt(jnp.sum(yg * yg, axis=-1, keepdims=True) + EPS)
        o_ref[:, sl] = (yg * jnp.where(is_qk, inv, 1.0)).astype(o_ref.dtype)


def gdn_conv(proj, conv_w, meta, col0, gw):
    t = proj.shape[0]
    tb = _pick(t, (512, 256, 128))
    tc = _pick(np.gcd(col0, gw), (512, 256, 128))
    hb = BF16_SUBLANES
    c0 = col0 // tc
    return pl.pallas_call(
        functools.partial(_conv_kernel, tb=tb, hb=hb, tc=tc, n_qk_blocks=2 * gw // tc),
        out_shape=jax.ShapeDtypeStruct((t, 3 * gw), BF16),
        grid_spec=pltpu.PrefetchScalarGridSpec(
            num_scalar_prefetch=1,
            grid=(t // tb, 3 * gw // tc),
            in_specs=[
                pl.BlockSpec((tb, tc), lambda i, j, m: (i, c0 + j)),
                pl.BlockSpec((hb, tc), lambda i, j, m: (jnp.maximum(i * (tb // hb) - 1, 0), c0 + j)),
                pl.BlockSpec((hb, tc), lambda i, j, m: (jnp.minimum((i + 1) * (tb // hb), t // hb - 1), c0 + j)),
                pl.BlockSpec((conv_w.shape[0], tc), lambda i, j, m: (0, j)),
            ],
            out_specs=pl.BlockSpec((tb, tc), lambda i, j, m: (i, j)),
            scratch_shapes=[pltpu.VMEM((tb + 2 * hb, tc), F32)],
        ),
        compiler_params=_params(("arbitrary", "arbitrary")),
        name="gdn_conv",
    )(meta, proj, proj, proj, conv_w)


GDN_ROWS = LANES
GDN_HEAD_GROUP = 8


def _gdn_gates(ab, alog_row, dtb_row):
    x = ab + dtb_row
    softplus = jnp.maximum(x, 0.0) + jnp.log(1.0 + jnp.exp(-jnp.abs(x)))
    return -jnp.exp(alog_row) * softplus, 1.0 / (1.0 + jnp.exp(-ab))


def _chunk_cumsum(g, reverse):
    C = GDN_CHUNK
    n = g.shape[0]
    pos = lax.broadcasted_iota(I32, g.shape, 0) & (C - 1)
    x = g
    s = 1
    while s < C:
        if reverse:
            x = x + jnp.where(pos < C - s, pltpu.roll(x, n - s, axis=0), 0.0)
        else:
            x = x + jnp.where(pos >= s, pltpu.roll(x, s, axis=0), 0.0)
        s *= 2
    return x


def _bdot(a, b):
    return jnp.dot(a.astype(BF16), b.astype(BF16), preferred_element_type=F32)


def _gdn_prep_kernel(q_ref, k_ref, v_ref, ab_ref, alog_ref, dtb_ref, *out_refs, n_heads):
    C = GDN_CHUNK
    n_chunks = q_ref.shape[0] // C
    g_all, beta_all = _gdn_gates(ab_ref[...], alog_ref[...], dtb_ref[...])
    gcs = []
    for reverse in (False, True):
        gc = _chunk_cumsum(g_all, reverse)
        gcs.append((gc, gc.T))

    ri = lax.broadcasted_iota(I32, (C, C), 0)
    ci = lax.broadcasted_iota(I32, (C, C), 1)
    eye = (ri == ci).astype(F32)
    masks = [(ri >= ci, ri > ci), (ri <= ci, ri < ci)]
    scale = HEAD_DIM ** -0.5
    nt = (((1,), (1,)), ((), ()))

    for c in range(n_chunks):
        rows = slice(c * C, (c + 1) * C)
        for h0 in range(0, n_heads, GDN_HEAD_GROUP):
            heads = range(h0, min(h0 + GDN_HEAD_GROUP, n_heads))
            items = [(h, d) for h in heads for d in (0, 1)]
            hs = {h: slice(h * HEAD_DIM, (h + 1) * HEAD_DIM) for h in heads}
            ks = {h: k_ref[rows, hs[h]].astype(F32) for h in heads}
            qs = {h: q_ref[rows, hs[h]].astype(F32) * scale for h in heads}
            a2 = {h: lax.dot_general(jnp.concatenate([ks[h], qs[h]], axis=0).astype(BF16),
                                     ks[h].astype(BF16), nt, preferred_element_type=F32)
                  for h in heads}
            beta, gcc, eg, tmat, lp = {}, {}, {}, {}, {}
            for (h, d) in items:
                gc, gct = gcs[d]
                b_col = 2 * n_heads * d
                g_col = b_col + n_heads
                incl, strict = masks[d]
                beta[h, d] = beta_all[rows, b_col + h:b_col + h + 1]
                gcc[h, d] = gc[rows, g_col + h:g_col + h + 1]
                gcr = gct[g_col + h:g_col + h + 1, rows]
                decay = jnp.where(incl, jnp.exp(jnp.where(incl, gcc[h, d] - gcr, 0.0)), 0.0)
                lmat = jnp.where(strict, a2[h][:C] * beta[h, d] * decay, 0.0)
                attn = jnp.where(incl, a2[h][C:] * decay, 0.0)
                out_refs[5 * d + 4][rows, h * C:(h + 1) * C] = attn.astype(BF16)
                tmat[h, d] = eye - lmat
                lp[h, d] = lmat
                eg[h, d] = jnp.exp(gcc[h, d])
            p = 2
            while p < C:
                for it in items:
                    lp[it] = _bdot(lp[it], lp[it])
                for it in items:
                    tmat[it] = tmat[it] + _bdot(tmat[it], lp[it])
                p *= 2
            for (h, d) in items:
                u_ref, w_ref, qg_ref, kdt_ref = out_refs[5 * d:5 * d + 4]
                kb = ks[h] * beta[h, d]
                vb = v_ref[rows, hs[h]].astype(F32) * beta[h, d]
                uw = _bdot(tmat[h, d], jnp.concatenate([vb, kb * eg[h, d]], axis=1))
                u_ref[rows, hs[h]] = uw[:, :HEAD_DIM].astype(BF16)
                w_ref[rows, hs[h]] = uw[:, HEAD_DIM:].astype(BF16)
                qg_ref[rows, hs[h]] = (qs[h] * eg[h, d]).astype(BF16)
                g_last_row = c * C if d else (c + 1) * C - 1
                g_col = 2 * n_heads * d + n_heads
                glast = gcs[d][0][g_last_row:g_last_row + 1, g_col + h:g_col + h + 1]
                kd = ks[h] * jnp.exp(glast - gcc[h, d])
                kdt_ref[c * HEAD_DIM:(c + 1) * HEAD_DIM, h * C:(h + 1) * C] = kd.T.astype(BF16)


def gdn_prep(qkv, ab, alog_row, dtb_row, gw):
    t = qkv.shape[0]
    nh = gw // HEAD_DIM
    rb = GDN_ROWS
    aw = nh * GDN_CHUNK
    assert aw % LANES == 0
    row_spec = lambda cb: pl.BlockSpec((rb, gw), lambda i: (i, cb))
    one_dir_shapes = [jax.ShapeDtypeStruct((t, gw), BF16)] * 3 + [
        jax.ShapeDtypeStruct((t // GDN_CHUNK * HEAD_DIM, aw), BF16), jax.ShapeDtypeStruct((t, aw), BF16)]
    one_dir_specs = [pl.BlockSpec((rb, gw), lambda i: (i, 0))] * 3 + [
        pl.BlockSpec((rb // GDN_CHUNK * HEAD_DIM, aw), lambda i: (i, 0)), pl.BlockSpec((rb, aw), lambda i: (i, 0))]
    return pl.pallas_call(
        functools.partial(_gdn_prep_kernel, n_heads=nh),
        out_shape=tuple(one_dir_shapes * 2),
        grid=(t // rb,),
        in_specs=[row_spec(0), row_spec(1), row_spec(2),
                  pl.BlockSpec((rb, LANES), lambda i: (i, 0)),
                  pl.BlockSpec((1, LANES), lambda i: (0, 0)),
                  pl.BlockSpec((1, LANES), lambda i: (0, 0))],
        out_specs=tuple(one_dir_specs * 2),
        compiler_params=_params(("arbitrary",)),
        name="gdn_prep",
    )(qkv, qkv, qkv, ab, alog_row, dtb_row)


def _gdn_scan_kernel(meta_ref, u_ref, w_ref, qg_ref, kdt_ref, at_ref, ab_ref, alog_ref, dtb_ref, *rest,
                     n_heads, reverse, finalize):
    if finalize:
        ofwd_ref, z_ref, nw_ref, o_ref, s_ref = rest
    else:
        o_ref, s_ref = rest
    C = GDN_CHUNK
    n_chunks = u_ref.shape[0] // C
    i = pl.program_id(0)

    @pl.when(meta_ref[0, i] == 1)
    def _():
        s_ref[...] = jnp.zeros_like(s_ref)

    g_all, _ = _gdn_gates(ab_ref[...], alog_ref[...], dtb_ref[...])
    g_col = (3 if reverse else 1) * n_heads
    heads = range(n_heads)
    hs = [slice(h * HEAD_DIM, (h + 1) * HEAD_DIM) for h in heads]
    order = range(n_chunks - 1, -1, -1) if reverse else range(n_chunks)
    for c in order:
        rows = slice(c * C, (c + 1) * C)
        chunk_decay = jnp.exp(jnp.sum(g_all[rows], axis=0, keepdims=True))
        wq = [jnp.dot(jnp.concatenate([w_ref[rows, hs[h]], qg_ref[rows, hs[h]]], axis=0),
                      s_ref[h].astype(BF16), preferred_element_type=F32) for h in heads]
        vn = [(u_ref[rows, hs[h]].astype(F32) - wq[h][:C]).astype(BF16) for h in heads]
        outs = [wq[h][C:] + jnp.dot(at_ref[rows, h * C:(h + 1) * C], vn[h], preferred_element_type=F32)
                for h in heads]
        for h in heads:
            s_ref[h] = (s_ref[h] * chunk_decay[:, g_col + h:g_col + h + 1]
                        + jnp.dot(kdt_ref[c * HEAD_DIM:(c + 1) * HEAD_DIM, h * C:(h + 1) * C], vn[h],
                                  preferred_element_type=F32))
        for h in heads:
            o = outs[h]
            if finalize:
                o = o + ofwd_ref[rows, hs[h]]
                o = o * lax.rsqrt(jnp.mean(o * o, axis=-1, keepdims=True) + EPS) * nw_ref[...]
                z = z_ref[rows, hs[h]].astype(F32)
                o = o * (z * (1.0 / (1.0 + jnp.exp(-z))))
            o_ref[rows, hs[h]] = o.astype(o_ref.dtype)


def gdn_scan(prep, ab, alog_row, dtb_row, meta, gw, reverse, ofwd=None, proj=None, z_col=None, norm_w=None):
    u, w, qg, kdt, attn = prep
    t = u.shape[0]
    nh = gw // HEAD_DIM
    rb = GDN_ROWS
    nb = t // rb
    aw = attn.shape[1]
    finalize = ofwd is not None
    idx = (lambda i, m: (nb - 1 - i)) if reverse else (lambda i, m: i)
    wide = pl.BlockSpec((rb, gw), lambda i, m: (idx(i, m), 0))
    in_specs = [wide, wide, wide,
                pl.BlockSpec((rb // GDN_CHUNK * HEAD_DIM, aw), lambda i, m: (idx(i, m), 0)),
                pl.BlockSpec((rb, aw), lambda i, m: (idx(i, m), 0)),
                pl.BlockSpec((rb, LANES), lambda i, m: (idx(i, m), 0)),
                pl.BlockSpec((1, LANES), lambda i, m: (0, 0)),
                pl.BlockSpec((1, LANES), lambda i, m: (0, 0))]
    args = [meta, u, w, qg, kdt, attn, ab, alog_row, dtb_row]
    if finalize:
        assert z_col % gw == 0
        in_specs += [wide,
                     pl.BlockSpec((rb, gw), lambda i, m: (idx(i, m), z_col // gw)),
                     pl.BlockSpec((1, HEAD_DIM), lambda i, m: (0, 0))]
        args += [ofwd, proj, norm_w]
    return pl.pallas_call(
        functools.partial(_gdn_scan_kernel, n_heads=nh, reverse=reverse, finalize=finalize),
        out_shape=jax.ShapeDtypeStruct((t, gw), BF16 if finalize else F32),
        grid_spec=pltpu.PrefetchScalarGridSpec(
            num_scalar_prefetch=1,
            grid=(nb,),
            in_specs=in_specs,
            out_specs=wide,
            scratch_shapes=[pltpu.VMEM((nh, HEAD_DIM, HEAD_DIM), F32)],
        ),
        compiler_params=_params(("arbitrary",)),
        name="gdn_scan_bwd" if reverse else "gdn_scan_fwd",
    )(*args)


def _outproj_kernel(oa_ref, og_ref, wa_ref, wg_ref, xp_ref, xs_ref, o_ref, *, nbp):
    i = pl.program_id(0)
    mixed = (jnp.dot(oa_ref[...], wa_ref[...], preferred_element_type=F32)
             + jnp.dot(og_ref[...], wg_ref[...], preferred_element_type=F32))

    @pl.when(i < nbp)
    def _():
        o_ref[...] = xp_ref[...] + mixed

    @pl.when(i >= nbp)
    def _():
        o_ref[...] = xs_ref[...] + mixed


def out_projection(oa, og, w_out, xp, xs):
    t, aw = oa.shape
    gw = og.shape[1]
    d = w_out.shape[1]
    tp = xp.shape[0]
    tm = _pick(np.gcd(tp, xs.shape[0]), (1024, 512, 256, 128))
    tn = _pick(d, (512, 256, 128))
    nbp = tp // tm
    assert aw % 16 == 0
    return pl.pallas_call(
        functools.partial(_outproj_kernel, nbp=nbp),
        out_shape=jax.ShapeDtypeStruct((t, d), F32),
        grid=(t // tm, d // tn),
        in_specs=[
            pl.BlockSpec((tm, aw), lambda i, j: (i, 0)),
            pl.BlockSpec((tm, gw), lambda i, j: (i, 0)),
            pl.BlockSpec((aw, tn), lambda i, j: (0, j)),
            pl.BlockSpec((gw, tn), lambda i, j: (aw // gw, j)),
            pl.BlockSpec((tm, tn), lambda i, j: (jnp.minimum(i, nbp - 1), jnp.where(i < nbp, j, 0))),
            pl.BlockSpec((tm, tn), lambda i, j: (jnp.maximum(i - nbp, 0), jnp.where(i >= nbp, j, 0))),
        ],
        out_specs=pl.BlockSpec((tm, tn), lambda i, j: (i, j)),
        compiler_params=_params(("arbitrary", "arbitrary")),
        name="out_proj",
    )(oa, og, w_out, w_out, xp, xs)


def _router_kernel(x_ref, g_ref, wr_ref, br_ref, h_ref, idx_ref, gate_ref):
    h = _rms(x_ref[...], g_ref[...])
    h_ref[...] = h
    logits = jnp.dot(h, wr_ref[...], precision=lax.Precision.HIGHEST,
                     preferred_element_type=F32) + br_ref[...]
    col = lax.broadcasted_iota(I32, logits.shape, 1).astype(F32)
    vals = logits
    tops, idxs = [], []
    for _ in range(TOP_K):
        m = jnp.max(vals, axis=-1, keepdims=True)
        idx = jnp.min(jnp.where(vals == m, col, float(LANES)), axis=-1, keepdims=True)
        tops.append(m)
        idxs.append(idx)
        vals = jnp.where(col == idx, -jnp.inf, vals)
    es = [jnp.exp(v - tops[0]) for v in tops]
    den = es[0]
    for e in es[1:]:
        den = den + e
    idx_out = jnp.zeros(logits.shape, F32)
    gate_out = jnp.zeros(logits.shape, F32)
    for k in range(TOP_K):
        idx_out = jnp.where(col == float(k), idxs[k], idx_out)
        gate_out = jnp.where(col == float(k), es[k] / den, gate_out)
    idx_ref[...] = idx_out.astype(I32)
    gate_ref[...] = gate_out


def ffn_norm_router(x1, g, w_router, b_router):
    t, d = x1.shape
    e = w_router.shape[1]
    assert TOP_K <= e <= LANES
    wr = jnp.zeros((d, LANES), F32).at[:, :e].set(w_router)
    br = jnp.full((1, LANES), NEG_INF, F32).at[0, :e].set(b_router)
    tr = _pick(t, (256, 128))
    return pl.pallas_call(
        _router_kernel,
        out_shape=(jax.ShapeDtypeStruct((t, d), F32),
                   jax.ShapeDtypeStruct((t, LANES), I32),
                   jax.ShapeDtypeStruct((t, LANES), F32)),
        grid=(t // tr,),
        in_specs=[
            pl.BlockSpec((tr, d), lambda i: (i, 0)),
            pl.BlockSpec((1, d), lambda i: (0, 0)),
            pl.BlockSpec((d, LANES), lambda i: (0, 0)),
            pl.BlockSpec((1, LANES), lambda i: (0, 0)),
        ],
        out_specs=(pl.BlockSpec((tr, d), lambda i: (i, 0)),
                   pl.BlockSpec((tr, LANES), lambda i: (i, 0)),
                   pl.BlockSpec((tr, LANES), lambda i: (i, 0))),
        compiler_params=_params(("arbitrary",)),
        name="ffn_norm_router",
    )(x1, g.reshape(1, d), wr, br)


def _gather_rows(tok_ref, src_hbm, dst, sem, n_rows, row0=0):
    def body(j, carry):
        r = row0 + j
        tok = tok_ref[0, 0, r]
        pltpu.make_async_copy(src_hbm.at[pl.ds(tok, 1)], dst.at[pl.ds(r, 1)], sem).start()
        return carry
    lax.fori_loop(0, n_rows, body, 0, unroll=GATHER_UNROLL)


def _wait_rows(src_hbm, dst, sem):
    pltpu.make_async_copy(src_hbm.at[pl.ds(0, dst.shape[0])], dst, sem).wait()


def _moe_up_kernel(be_ref, nv_ref, tokc_ref, tokn_ref, x_hbm, wg_ref, bg_ref, wu_ref, bu_ref, h_ref,
                   xg, xb, sem, *, tm, nt):
    m = pl.program_id(0)
    n = pl.program_id(1)
    nvalid = nv_ref[0]
    live = m < nvalid
    rows_per_step = tm // nt

    @pl.when(live & (n == 0))
    def _():
        @pl.when(m == 0)
        def _():
            _gather_rows(tokc_ref, x_hbm, xg, sem, tm)
        _wait_rows(x_hbm, xg, sem)
        xb[...] = xg[...].astype(BF16)

    @pl.when(live)
    def _():
        row0 = n * rows_per_step
        for j in range(rows_per_step):
            tok = tokn_ref[0, 0, row0 + j]
            pltpu.make_async_copy(x_hbm.at[pl.ds(tok, 1)], xg.at[pl.ds(row0 + j, 1)], sem).start()
        x = xb[...]
        gate = jnp.dot(x, wg_ref[...].astype(BF16), preferred_element_type=F32) + bg_ref[...]
        up = jnp.dot(x, wu_ref[...].astype(BF16), preferred_element_type=F32) + bu_ref[...]
        gate = jnp.minimum(gate, SWIGLU_LIMIT)
        up = jnp.clip(up, -SWIGLU_LIMIT, SWIGLU_LIMIT)
        glu = gate * (1.0 / (1.0 + jnp.exp(-gate * SWIGLU_ALPHA)))
        h_ref[...] = ((up + 1.0) * glu).astype(h_ref.dtype)

    @pl.when((m == nvalid - 1) & (n == nt - 1))
    def _():
        _wait_rows(x_hbm, xg, sem)

    @pl.when(jnp.logical_not(live))
    def _():
        h_ref[...] = jnp.zeros_like(h_ref)


def _moe_up_col_tile(f):
    return _pick(f, (256, 128))


def _moe_row_block(n_assign, n_experts, f):
    mean = n_assign / n_experts
    sd = (mean * (1.0 - 1.0 / n_experts)) ** 0.5
    k = max(1, round(mean / MOE_ROW_BLOCK))
    step = int(np.lcm(BF16_SUBLANES, f // _moe_up_col_tile(f)))
    return int(-(-(mean + 2.0 * sd) // (k * step)) * step)


def moe_up(h2, row_tok3, block_e, nvalid, w_gate, b_gate, w_up, b_up, tm):
    n_blocks = row_tok3.shape[0]
    e, d, f = w_gate.shape
    tn = _moe_up_col_tile(f)
    nt = f // tn
    assert tm % nt == 0
    wmap = lambda m, n, be, nv: (be[m], 0, jnp.where(m < nv[0], n, nt - 1))
    return pl.pallas_call(
        functools.partial(_moe_up_kernel, tm=tm, nt=nt),
        out_shape=jax.ShapeDtypeStruct((n_blocks * tm, f), BF16),
        grid_spec=pltpu.PrefetchScalarGridSpec(
            num_scalar_prefetch=2,
            grid=(n_blocks, nt),
            in_specs=[
                pl.BlockSpec((1, 1, tm), lambda m, n, be, nv: (m, 0, 0), memory_space=pltpu.SMEM),
                pl.BlockSpec((1, 1, tm), lambda m, n, be, nv: (jnp.minimum(m + 1, n_blocks - 1), 0, 0),
                             memory_space=pltpu.SMEM),
                pl.BlockSpec(memory_space=pl.ANY),
                pl.BlockSpec((None, d, tn), wmap),
                pl.BlockSpec((None, 1, tn), wmap),
                pl.BlockSpec((None, d, tn), wmap),
                pl.BlockSpec((None, 1, tn), wmap),
            ],
            out_specs=pl.BlockSpec((tm, tn), lambda m, n, be, nv: (m, n)),
            scratch_shapes=[pltpu.VMEM((tm, d), F32), pltpu.VMEM((tm, d), BF16),
                            pltpu.SemaphoreType.DMA(())],
        ),
        compiler_params=_params(("arbitrary", "arbitrary")),
        name="moe_up",
    )(block_e, nvalid, row_tok3, row_tok3, h2, w_gate, b_gate.reshape(e, 1, f), w_up, b_up.reshape(e, 1, f))


def _moe_down_kernel(be_ref, nv_ref, h_ref, wd_ref, bd_ref, o_ref):
    m = pl.program_id(0)
    live = m < nv_ref[0]

    @pl.when(live)
    def _():
        o_ref[...] = jnp.dot(h_ref[...], wd_ref[...].astype(BF16), preferred_element_type=F32) + bd_ref[...]

    @pl.when(jnp.logical_not(live))
    def _():
        o_ref[...] = jnp.zeros_like(o_ref)


def moe_down(hmid, block_e, nvalid, w_down, b_down, tm):
    n_rows, f = hmid.shape
    e, _, d = w_down.shape
    n_blocks = n_rows // tm
    tn = _pick(d, (512, 256, 128))
    nt = d // tn
    wmap = lambda m, n, be, nv: (be[m], 0, jnp.where(m < nv[0], n, nt - 1))
    return pl.pallas_call(
        _moe_down_kernel,
        out_shape=jax.ShapeDtypeStruct((n_rows, d), F32),
        grid_spec=pltpu.PrefetchScalarGridSpec(
            num_scalar_prefetch=2,
            grid=(n_blocks, nt),
            in_specs=[
                pl.BlockSpec((tm, f), lambda m, n, be, nv: (m, 0)),
                pl.BlockSpec((None, f, tn), wmap),
                pl.BlockSpec((None, 1, tn), wmap),
            ],
            out_specs=pl.BlockSpec((tm, tn), lambda m, n, be, nv: (m, n)),
        ),
        compiler_params=_params(("arbitrary", "arbitrary")),
        name="moe_down",
    )(block_e, nvalid, hmid, w_down, b_down.reshape(e, 1, d))


def _combine_kernel(destc_ref, destn_ref, rows_hbm, x_ref, gate_ref, g_ref, o_ref, buf, sem, *, tr, nb):
    i = pl.program_id(0)
    slot = i % 2
    n_rows = TOP_K * tr

    @pl.when(i == 0)
    def _():
        _gather_rows(destc_ref, rows_hbm, buf.at[0], sem.at[0], n_rows)

    @pl.when(i + 1 < nb)
    def _():
        _gather_rows(destn_ref, rows_hbm, buf.at[1 - slot], sem.at[1 - slot], n_rows)

    _wait_rows(rows_hbm, buf.at[slot], sem.at[slot])
    y = x_ref[...]
    gates = gate_ref[...]
    for k in range(TOP_K):
        y = y + gates[:, k:k + 1] * buf[slot, pl.ds(k * tr, tr), :]
    o_ref[...] = _rms(y, g_ref[...])


def moe_combine_norm(x1, rows, dest3, gates, g, row0, n_tok, tr):
    d = x1.shape[1]
    nb = n_tok // tr
    b0 = row0 // tr
    nbt = dest3.shape[0]
    return pl.pallas_call(
        functools.partial(_combine_kernel, tr=tr, nb=nb),
        out_shape=jax.ShapeDtypeStruct((n_tok, d), F32),
        grid=(nb,),
        in_specs=[
            pl.BlockSpec((1, 1, TOP_K * tr), lambda i: (b0 + i, 0, 0), memory_space=pltpu.SMEM),
            pl.BlockSpec((1, 1, TOP_K * tr), lambda i: (jnp.minimum(b0 + i + 1, nbt - 1), 0, 0),
                         memory_space=pltpu.SMEM),
            pl.BlockSpec(memory_space=pl.ANY),
            pl.BlockSpec((tr, d), lambda i: (b0 + i, 0)),
            pl.BlockSpec((tr, LANES), lambda i: (b0 + i, 0)),
            pl.BlockSpec((1, d), lambda i: (0, 0)),
        ],
        out_specs=pl.BlockSpec((tr, d), lambda i: (i, 0)),
        scratch_shapes=[pltpu.VMEM((2, TOP_K * tr, d), F32), pltpu.SemaphoreType.DMA((2,))],
        compiler_params=_params(("arbitrary",)),
        name="moe_combine_norm",
    )(dest3, dest3, rows, x1, gates, g.reshape(1, d))


def _route(top_idx, n_experts, tm):
    t = top_idx.shape[0]
    n_assign = t * TOP_K
    flat_e = top_idx.reshape(-1)
    onehot = (flat_e[:, None] == jnp.arange(n_experts, dtype=I32)[None, :]).astype(I32)
    csum = jnp.cumsum(onehot, axis=0)
    counts = csum[-1]
    rank = jnp.take_along_axis(csum, flat_e[:, None], axis=1)[:, 0] - 1
    padded = ((counts + tm - 1) // tm) * tm
    pend = jnp.cumsum(padded)
    pstart = pend - padded
    dest = pstart[flat_e] + rank
    n_blocks = -(-n_assign // tm) + n_experts
    flat_tok = jnp.arange(n_assign, dtype=I32) // TOP_K
    row_tok = jnp.zeros((n_blocks * tm,), I32).at[dest].set(flat_tok, unique_indices=True)
    block_e = jnp.minimum(
        jnp.searchsorted(pend, jnp.arange(n_blocks, dtype=I32) * tm, side='right'), n_experts - 1).astype(I32)
    nvalid = (pend[-1] // tm).astype(I32).reshape(1)
    return row_tok, dest.reshape(t, TOP_K).astype(I32), block_e, nvalid


def _seq_table(seqs, blk):
    pos0, slen, first, last = [], [], [], []
    for (b, s) in seqs:
        assert s % blk == 0
        nb = s // blk
        for _ in range(b):
            for j in range(nb):
                pos0.append(j * blk)
                slen.append(s)
                first.append(int(j == 0))
                last.append(int(j == nb - 1))
    return np.array(pos0, np.int32), np.array(slen, np.int32), np.array(first, np.int32), np.array(last, np.int32)


def _rope_tables(seqs):
    half = HEAD_DIM // 2
    inv = ROPE_THETA ** (-jnp.arange(0, HEAD_DIM, 2, dtype=F32) / HEAD_DIM)
    cos_l, sin_l = [], []
    for (b, s) in seqs:
        ang = jnp.arange(s, dtype=F32)[:, None] * inv[None, :]
        c = jnp.cos(ang)
        sn = jnp.sin(ang)
        cos_l.append(jnp.tile(jnp.concatenate([c, c], axis=-1), (b, 1)))
        sin_l.append(jnp.tile(jnp.concatenate([-sn, sn], axis=-1), (b, 1)))
    del half
    return jnp.concatenate(cos_l, axis=0), jnp.concatenate(sin_l, axis=0)


def _layer(xp, xs, seqs, norm_mix, w_in, conv_w, a_log_fwd, dt_bias_fwd, a_log_bwd, dt_bias_bwd,
           gdn_norm, w_out, norm_ffn, w_router, b_router, w_gate, b_gate, w_up, b_up, w_down, b_down,
           norm_final):
    comb_tr, attn_tq = COMBINE_TOKEN_BLOCK, ATTN_Q_BLOCK
    d = xp.shape[1]
    nhg = a_log_fwd.shape[0]
    gw = nhg * HEAD_DIM
    aw = w_out.shape[0] - gw
    n_main = 3 * aw + 4 * gw
    assert w_in.shape[1] == n_main + 4 * nhg and 4 * nhg <= LANES
    t = xp.shape[0] + xs.shape[0]

    h = norm_two_streams(xp, xs, norm_mix)
    w_main = w_in[:, :n_main].astype(BF16)
    w_tail = jnp.zeros((d, LANES), BF16).at[:, :4 * nhg].set(w_in[:, n_main:].astype(BF16))
    proj, ab = in_projection(h, w_main, w_tail)

    cos, sin = _rope_tables(seqs)
    qk = rope_qk(proj, cos, sin, aw)
    pos0, slen, _, _ = _seq_table(seqs, attn_tq)
    o_attn = dilated_attention(qk, proj, jnp.asarray(np.stack([pos0, slen])), aw, attn_tq)

    conv_tb = _pick(t, (512, 256, 128))
    _, _, cfirst, clast = _seq_table(seqs, conv_tb)
    qkv = gdn_conv(proj, conv_w, jnp.asarray(np.stack([cfirst, clast])), 3 * aw, gw)
    _, _, gfirst, glast = _seq_table(seqs, LANES)
    zrow = jnp.zeros((nhg,), F32)
    alog_row = jnp.zeros((1, LANES), F32).at[0, :4 * nhg].set(
        jnp.concatenate([zrow, a_log_fwd, zrow, a_log_bwd]))
    dtb_row = jnp.zeros((1, LANES), F32).at[0, :4 * nhg].set(
        jnp.concatenate([zrow, dt_bias_fwd, zrow, dt_bias_bwd]))
    prep = gdn_prep(qkv, ab, alog_row, dtb_row, gw)
    o_fwd = gdn_scan(prep[:5], ab, alog_row, dtb_row, jnp.asarray(gfirst[None, :]), gw, reverse=False)
    o_gdn = gdn_scan(prep[5:], ab, alog_row, dtb_row, jnp.asarray(glast[::-1][None, :].copy()), gw, reverse=True,
                     ofwd=o_fwd, proj=proj, z_col=3 * aw + 3 * gw, norm_w=gdn_norm.reshape(1, HEAD_DIM))

    x1 = out_projection(o_attn, o_gdn, w_out.astype(BF16), xp, xs)

    h2, idx_pad, gate_pad = ffn_norm_router(x1, norm_ffn, w_router, b_router)
    n_experts = w_router.shape[1]
    moe_tm = _moe_row_block(t * TOP_K, n_experts, w_gate.shape[2])
    row_tok, dest, block_e, nvalid = _route(idx_pad[:, :TOP_K], n_experts, moe_tm)
    n_blocks = block_e.shape[0]
    hmid = moe_up(h2, row_tok.reshape(n_blocks, 1, moe_tm), block_e, nvalid, w_gate, b_gate, w_up, b_up, moe_tm)
    rows = moe_down(hmid, block_e, nvalid, w_down, b_down, moe_tm)
    dest3 = dest.reshape(t // comb_tr, 1, comb_tr, TOP_K).transpose(0, 1, 3, 2).reshape(t // comb_tr, 1, TOP_K * comb_tr)
    tp = xp.shape[0]
    yp = moe_combine_norm(x1, rows, dest3, gate_pad, norm_final, 0, tp, comb_tr)
    ys = moe_combine_norm(x1, rows, dest3, gate_pad, norm_final, tp, xs.shape[0], comb_tr)
    return yp, ys


def kernel(x_prompt, x_sample, norm_mix, w_in, conv_w, a_log_fwd, dt_bias_fwd, a_log_bwd, dt_bias_bwd,
           gdn_norm, w_out, norm_ffn, w_router, b_router, w_gate, b_gate, w_up, b_up, w_down, b_down,
           norm_final):
    assert norm_mix.shape[0] == 1, "single-layer encoder"
    bp, sp, d = x_prompt.shape
    bs, ss, _ = x_sample.shape
    seqs = ((bp, sp), (bs, ss))
    layer0 = lambda a: a.reshape(a.shape[1:])
    yp, ys = _layer(
        x_prompt.reshape(bp * sp, d), x_sample.reshape(bs * ss, d), seqs,
        *[layer0(a) for a in (norm_mix, w_in, conv_w, a_log_fwd, dt_bias_fwd, a_log_bwd, dt_bias_bwd,
                              gdn_norm, w_out, norm_ffn, w_router, b_router, w_gate, b_gate, w_up,
                              b_up, w_down, b_down)],
        norm_final)
    return yp.reshape(bp, sp, d), ys.reshape(bs, ss, d)
```

```python
import functools

import numpy as np
import jax
import jax.numpy as jnp
from jax import lax
from jax.experimental import pallas as pl
from jax.experimental.pallas import tpu as pltpu

F32 = jnp.float32
BF16 = jnp.bfloat16
I32 = jnp.int32

HEAD_DIM = 128
DILATED_BRANCHES = ((128, 1), (512, 4), (2048, 16))
ATTN_SIDE = 64
ATTN_CHUNK = 2 * ATTN_SIDE
ROPE_THETA = 10000.0
GDN_CHUNK = 64
TOP_K = 4
SWIGLU_LIMIT = 7.0
SWIGLU_ALPHA = 1.702
EPS = 1e-6
NEG_INF = -1e30
LANES = 128
BF16_SUBLANES = 16
VMEM_LIMIT = 56 * 1024 * 1024
MOE_ROW_BLOCK = 1024
COMBINE_TOKEN_BLOCK = 128
ATTN_Q_BLOCK = 2048
GATHER_UNROLL = 8
ATTN_UNROLL = 8


def _pick(n, cands):
    for c in cands:
        if n % c == 0:
            return c
    raise ValueError(f"no tile in {cands} divides {n}")


def _params(sem, vmem=VMEM_LIMIT):
    return pltpu.CompilerParams(dimension_semantics=sem, vmem_limit_bytes=vmem)


def _rms(x, g):
    ms = jnp.mean(x * x, axis=-1, keepdims=True)
    return x * lax.rsqrt(ms + EPS) * g


def _norm2_kernel(xp_ref, xs_ref, g_ref, o_ref, *, nbp):
    i = pl.program_id(0)

    @pl.when(i < nbp)
    def _():
        o_ref[...] = _rms(xp_ref[...], g_ref[...]).astype(o_ref.dtype)

    @pl.when(i >= nbp)
    def _():
        o_ref[...] = _rms(xs_ref[...], g_ref[...]).astype(o_ref.dtype)


def norm_two_streams(xp, xs, g):
    tp, d = xp.shape
    ts = xs.shape[0]
    tr = _pick(np.gcd(tp, ts), (256, 128, 64, 32, 16))
    nbp = tp // tr
    nb = nbp + ts // tr
    return pl.pallas_call(
        functools.partial(_norm2_kernel, nbp=nbp),
        out_shape=jax.ShapeDtypeStruct((tp + ts, d), BF16),
        grid=(nb,),
        in_specs=[
            pl.BlockSpec((tr, d), lambda i: (jnp.minimum(i, nbp - 1), 0)),
            pl.BlockSpec((tr, d), lambda i: (jnp.maximum(i - nbp, 0), 0)),
            pl.BlockSpec((1, d), lambda i: (0, 0)),
        ],
        out_specs=pl.BlockSpec((tr, d), lambda i: (i, 0)),
        compiler_params=_params(("arbitrary",)),
        name="norm_mix",
    )(xp, xs, g.reshape(1, d))


def _inproj_kernel(h_ref, w_ref, wt_ref, o_ref, ab_ref):
    j = pl.program_id(1)
    o_ref[...] = jnp.dot(h_ref[...], w_ref[...], preferred_element_type=F32).astype(o_ref.dtype)

    @pl.when(j == 0)
    def _():
        ab_ref[...] = jnp.dot(h_ref[...], wt_ref[...], preferred_element_type=F32)


def in_projection(h, w_main, w_tail):
    t, d = h.shape
    n = w_main.shape[1]
    tm = _pick(t, (1024, 512, 256, 128))
    tn = _pick(n, (512, 256, 128))
    return pl.pallas_call(
        _inproj_kernel,
        out_shape=(jax.ShapeDtypeStruct((t, n), BF16), jax.ShapeDtypeStruct((t, LANES), F32)),
        grid=(t // tm, n // tn),
        in_specs=[
            pl.BlockSpec((tm, d), lambda i, j: (i, 0)),
            pl.BlockSpec((d, tn), lambda i, j: (0, j)),
            pl.BlockSpec((d, LANES), lambda i, j: (0, 0)),
        ],
        out_specs=(
            pl.BlockSpec((tm, tn), lambda i, j: (i, j)),
            pl.BlockSpec((tm, LANES), lambda i, j: (i, 0)),
        ),
        compiler_params=_params(("arbitrary", "arbitrary")),
        name="in_proj",
    )(h, w_main, w_tail)


def _rope_kernel(x_ref, cos_ref, sin_ref, o_ref, *, n_heads):
    cos = cos_ref[...]
    sin = sin_ref[...]
    for h in range(n_heads):
        sl = slice(h * HEAD_DIM, (h + 1) * HEAD_DIM)
        x = x_ref[:, sl].astype(F32)
        rot = pltpu.roll(x, HEAD_DIM // 2, axis=1)
        o_ref[:, sl] = (x * cos + rot * sin).astype(o_ref.dtype)


def rope_qk(proj, cos, sin, aw):
    t = proj.shape[0]
    tr = _pick(t, (1024, 512, 256))
    return pl.pallas_call(
        functools.partial(_rope_kernel, n_heads=aw // HEAD_DIM),
        out_shape=jax.ShapeDtypeStruct((t, 2 * aw), BF16),
        grid=(t // tr, 2),
        in_specs=[
            pl.BlockSpec((tr, aw), lambda i, j: (i, j)),
            pl.BlockSpec((tr, HEAD_DIM), lambda i, j: (i, 0)),
            pl.BlockSpec((tr, HEAD_DIM), lambda i, j: (i, 0)),
        ],
        out_specs=pl.BlockSpec((tr, aw), lambda i, j: (i, j)),
        compiler_params=_params(("arbitrary", "arbitrary")),
        name="rope_qk",
    )(proj, cos, sin)


def _attn_kernel(meta_ref, q_ref, kc_ref, kp_ref, kn_ref, vc_ref, vp_ref, vn_ref, o_ref,
                 qf, kw, vw, ob, lb, *, tq, halo):
    i = pl.program_id(0)
    pos0 = meta_ref[0, i]
    slen = meta_ref[1, i]
    qf[...] = q_ref[...].astype(F32)
    kw[0:halo] = kp_ref[...].astype(F32)
    kw[halo:halo + tq] = kc_ref[...].astype(F32)
    kw[halo + tq:] = kn_ref[...].astype(F32)
    vw[0:halo] = vp_ref[...].astype(F32)
    vw[halo:halo + tq] = vc_ref[...].astype(F32)
    vw[halo + tq:] = vn_ref[...].astype(F32)

    scale = HEAD_DIM ** -0.5
    nk = 2 * ATTN_CHUNK
    a_idx = lax.broadcasted_iota(I32, (ATTN_CHUNK, nk), 0)
    j_idx = lax.broadcasted_iota(I32, (ATTN_CHUNK, nk), 1)
    rel = j_idx - a_idx
    band = (rel >= 0) & (rel <= 2 * ATTN_SIDE)

    for b, (_, dil) in enumerate(DILATED_BRANCHES):
        def body(c, carry, b=b, dil=dil):
            r = c % dil
            start = r + (c // dil) * (dil * ATTN_CHUNK)
            q = qf[pl.ds(start, ATTN_CHUNK, stride=dil), :]
            kstart = halo + start - ATTN_SIDE * dil
            k = kw[pl.ds(kstart, nk, stride=dil), :]
            v = vw[pl.ds(kstart, nk, stride=dil), :]
            s = lax.dot_general(q.astype(BF16), k.astype(BF16), (((1,), (1,)), ((), ())),
                                preferred_element_type=F32) * scale
            kpos = (pos0 + start) + (j_idx - ATTN_SIDE) * dil
            valid = band & (kpos >= 0) & (kpos < slen)
            s = jnp.where(valid, s, NEG_INF)
            m = jnp.max(s, axis=-1, keepdims=True)
            p = jnp.exp(s - m)
            l = jnp.sum(p, axis=-1, keepdims=True)
            o = jnp.dot(p.astype(BF16), v.astype(BF16), preferred_element_type=F32) / l
            lse = m + jnp.log(l)
            ob[b, pl.ds(start, ATTN_CHUNK, stride=dil), :] = o
            lb[b, pl.ds(start, ATTN_CHUNK, stride=dil), :] = jnp.broadcast_to(lse, (ATTN_CHUNK, HEAD_DIM))
            return carry

        lax.fori_loop(0, tq // ATTN_CHUNK, body, 0, unroll=ATTN_UNROLL)

    l0, l1, l2 = lb[0], lb[1], lb[2]
    mx = jnp.maximum(jnp.maximum(l0, l1), l2)
    w0, w1, w2 = jnp.exp(l0 - mx), jnp.exp(l1 - mx), jnp.exp(l2 - mx)
    o = (w0 * ob[0] + w1 * ob[1] + w2 * ob[2]) / (w0 + w1 + w2)
    o_ref[...] = o.astype(o_ref.dtype)


def dilated_attention(qk, proj, meta, aw, tq):
    t = qk.shape[0]
    nh = aw // HEAD_DIM
    halo = ATTN_SIDE * max(d for _, d in DILATED_BRANCHES)
    assert tq % halo == 0
    hpb = tq // halo
    nhb = t // halo
    cur = lambda off: (lambda i, h, m: (i, off + h))
    prv = lambda off: (lambda i, h, m: (jnp.maximum(i * hpb - 1, 0), off + h))
    nxt = lambda off: (lambda i, h, m: (jnp.minimum((i + 1) * hpb, nhb - 1), off + h))
    grid_spec = pltpu.PrefetchScalarGridSpec(
        num_scalar_prefetch=1,
        grid=(t // tq, nh),
        in_specs=[
            pl.BlockSpec((tq, HEAD_DIM), cur(0)),
            pl.BlockSpec((tq, HEAD_DIM), cur(nh)),
            pl.BlockSpec((halo, HEAD_DIM), prv(nh)),
            pl.BlockSpec((halo, HEAD_DIM), nxt(nh)),
            pl.BlockSpec((tq, HEAD_DIM), cur(2 * nh)),
            pl.BlockSpec((halo, HEAD_DIM), prv(2 * nh)),
            pl.BlockSpec((halo, HEAD_DIM), nxt(2 * nh)),
        ],
        out_specs=pl.BlockSpec((tq, HEAD_DIM), lambda i, h, m: (i, h)),
        scratch_shapes=[
            pltpu.VMEM((tq, HEAD_DIM), F32),
            pltpu.VMEM((tq + 2 * halo, HEAD_DIM), F32),
            pltpu.VMEM((tq + 2 * halo, HEAD_DIM), F32),
            pltpu.VMEM((3, tq, HEAD_DIM), F32),
            pltpu.VMEM((3, tq, HEAD_DIM), F32),
        ],
    )
    return pl.pallas_call(
        functools.partial(_attn_kernel, tq=tq, halo=halo),
        out_shape=jax.ShapeDtypeStruct((t, aw), BF16),
        grid_spec=grid_spec,
        compiler_params=_params(("arbitrary", "arbitrary")),
        name="dilated_attn",
    )(meta, qk, qk, qk, qk, proj, proj, proj)


def _conv_kernel(meta_ref, xc_ref, xp_ref, xn_ref, w_ref, o_ref, xs, *, tb, hb, tc, n_qk_blocks):
    i = pl.program_id(0)
    j = pl.program_id(1)
    first = meta_ref[0, i] == 1
    last = meta_ref[1, i] == 1
    xs[hb:hb + tb] = xc_ref[...].astype(F32)
    xs[0:hb] = jnp.where(first, 0.0, xp_ref[...].astype(F32))
    xs[hb + tb:] = jnp.where(last, 0.0, xn_ref[...].astype(F32))
    cw = w_ref.shape[0]
    pad = cw // 2
    acc = w_ref[0:1, :] * xs[pl.ds(hb - pad, tb), :]
    for tap in range(1, cw):
        acc = acc + w_ref[tap:tap + 1, :] * xs[pl.ds(hb - pad + tap, tb), :]
    y = acc * (1.0 / (1.0 + jnp.exp(-acc)))
    is_qk = j < n_qk_blocks
    for g in range(tc // HEAD_DIM):
        sl = slice(g * HEAD_DIM, (g + 1) * HEAD_DIM)
        yg = y[:, sl]
        inv = lax.rsqrt(jnp.sum(yg * yg, axis=-1, keepdims=True) + EPS)
        o_ref[:, sl] = (yg * jnp.where(is_qk, inv, 1.0)).astype(o_ref.dtype)


def gdn_conv(proj, conv_w, meta, col0, gw):
    t = proj.shape[0]
    tb = _pick(t, (512, 256, 128))
    tc = _pick(np.gcd(col0, gw), (512, 256, 128))
    hb = BF16_SUBLANES
    c0 = col0 // tc
    return pl.pallas_call(
        functools.partial(_conv_kernel, tb=tb, hb=hb, tc=tc, n_qk_blocks=2 * gw // tc),
        out_shape=jax.ShapeDtypeStruct((t, 3 * gw), BF16),
        grid_spec=pltpu.PrefetchScalarGridSpec(
            num_scalar_prefetch=1,
            grid=(t // tb, 3 * gw // tc),
            in_specs=[
                pl.BlockSpec((tb, tc), lambda i, j, m: (i, c0 + j)),
                pl.BlockSpec((hb, tc), lambda i, j, m: (jnp.maximum(i * (tb // hb) - 1, 0), c0 + j)),
                pl.BlockSpec((hb, tc), lambda i, j, m: (jnp.minimum((i + 1) * (tb // hb), t // hb - 1), c0 + j)),
                pl.BlockSpec((conv_w.shape[0], tc), lambda i, j, m: (0, j)),
            ],
            out_specs=pl.BlockSpec((tb, tc), lambda i, j, m: (i, j)),
            scratch_shapes=[pltpu.VMEM((tb + 2 * hb, tc), F32)],
        ),
        compiler_params=_params(("arbitrary", "arbitrary")),
        name="gdn_conv",
    )(meta, proj, proj, proj, conv_w)


GDN_ROWS = LANES
GDN_HEAD_GROUP = 4


def _gdn_gates(ab, alog_row, dtb_row):
    x = ab + dtb_row
    softplus = jnp.maximum(x, 0.0) + jnp.log(1.0 + jnp.exp(-jnp.abs(x)))
    return -jnp.exp(alog_row) * softplus, 1.0 / (1.0 + jnp.exp(-ab))


def _chunk_cumsum(g, reverse):
    C = GDN_CHUNK
    n = g.shape[0]
    pos = lax.broadcasted_iota(I32, g.shape, 0) & (C - 1)
    x = g
    s = 1
    while s < C:
        if reverse:
            x = x + jnp.where(pos < C - s, pltpu.roll(x, n - s, axis=0), 0.0)
        else:
            x = x + jnp.where(pos >= s, pltpu.roll(x, s, axis=0), 0.0)
        s *= 2
    return x


def _bdot(a, b):
    return jnp.dot(a.astype(BF16), b.astype(BF16), preferred_element_type=F32)


def _gdn_prep_kernel(q_ref, k_ref, v_ref, ab_ref, alog_ref, dtb_ref, *out_refs, n_heads):
    C = GDN_CHUNK
    n_chunks = q_ref.shape[0] // C
    g_all, beta_all = _gdn_gates(ab_ref[...], alog_ref[...], dtb_ref[...])
    gcs = []
    for reverse in (False, True):
        gc = _chunk_cumsum(g_all, reverse)
        gcs.append((gc, gc.T))

    ri = lax.broadcasted_iota(I32, (C, C), 0)
    ci = lax.broadcasted_iota(I32, (C, C), 1)
    eye = (ri == ci).astype(F32)
    masks = [(ri >= ci, ri > ci), (ri <= ci, ri < ci)]
    scale = HEAD_DIM ** -0.5
    nt = (((1,), (1,)), ((), ()))

    for c in range(n_chunks):
        rows = slice(c * C, (c + 1) * C)
        for h0 in range(0, n_heads, GDN_HEAD_GROUP):
            heads = range(h0, min(h0 + GDN_HEAD_GROUP, n_heads))
            items = [(h, d) for h in heads for d in (0, 1)]
            hs = {h: slice(h * HEAD_DIM, (h + 1) * HEAD_DIM) for h in heads}
            ks = {h: k_ref[rows, hs[h]].astype(F32) for h in heads}
            qs = {h: q_ref[rows, hs[h]].astype(F32) * scale for h in heads}
            a2 = {h: lax.dot_general(jnp.concatenate([ks[h], qs[h]], axis=0).astype(BF16),
                                     ks[h].astype(BF16), nt, preferred_element_type=F32)
                  for h in heads}
            beta, gcc, eg, tmat, lp = {}, {}, {}, {}, {}
            for (h, d) in items:
                gc, gct = gcs[d]
                b_col = 2 * n_heads * d
                g_col = b_col + n_heads
                incl, strict = masks[d]
                beta[h, d] = beta_all[rows, b_col + h:b_col + h + 1]
                gcc[h, d] = gc[rows, g_col + h:g_col + h + 1]
                gcr = gct[g_col + h:g_col + h + 1, rows]
                decay = jnp.where(incl, jnp.exp(jnp.where(incl, gcc[h, d] - gcr, 0.0)), 0.0)
                lmat = jnp.where(strict, a2[h][:C] * beta[h, d] * decay, 0.0)
                attn = jnp.where(incl, a2[h][C:] * decay, 0.0)
                out_refs[5 * d + 4][rows, h * C:(h + 1) * C] = attn.astype(BF16)
                tmat[h, d] = eye - lmat
                lp[h, d] = lmat
                eg[h, d] = jnp.exp(gcc[h, d])
            p = 2
            while p < C:
                for it in items:
                    lp[it] = _bdot(lp[it], lp[it])
                for it in items:
                    tmat[it] = tmat[it] + _bdot(tmat[it], lp[it])
                p *= 2
            for (h, d) in items:
                u_ref, w_ref, qg_ref, kdt_ref = out_refs[5 * d:5 * d + 4]
                kb = ks[h] * beta[h, d]
                vb = v_ref[rows, hs[h]].astype(F32) * beta[h, d]
                uw = _bdot(tmat[h, d], jnp.concatenate([vb, kb * eg[h, d]], axis=1))
                u_ref[rows, hs[h]] = uw[:, :HEAD_DIM].astype(BF16)
                w_ref[rows, hs[h]] = uw[:, HEAD_DIM:].astype(BF16)
                qg_ref[rows, hs[h]] = (qs[h] * eg[h, d]).astype(BF16)
                g_last_row = c * C if d else (c + 1) * C - 1
                g_col = 2 * n_heads * d + n_heads
                glast = gcs[d][0][g_last_row:g_last_row + 1, g_col + h:g_col + h + 1]
                kd = ks[h] * jnp.exp(glast - gcc[h, d])
                kdt_ref[c * HEAD_DIM:(c + 1) * HEAD_DIM, h * C:(h + 1) * C] = kd.T.astype(BF16)


def gdn_prep(qkv, ab, alog_row, dtb_row, gw):
    t = qkv.shape[0]
    nh = gw // HEAD_DIM
    rb = GDN_ROWS
    aw = nh * GDN_CHUNK
    assert aw % LANES == 0
    row_spec = lambda cb: pl.BlockSpec((rb, gw), lambda i: (i, cb))
    one_dir_shapes = [jax.ShapeDtypeStruct((t, gw), BF16)] * 3 + [
        jax.ShapeDtypeStruct((t // GDN_CHUNK * HEAD_DIM, aw), BF16), jax.ShapeDtypeStruct((t, aw), BF16)]
    one_dir_specs = [pl.BlockSpec((rb, gw), lambda i: (i, 0))] * 3 + [
        pl.BlockSpec((rb // GDN_CHUNK * HEAD_DIM, aw), lambda i: (i, 0)), pl.BlockSpec((rb, aw), lambda i: (i, 0))]
    return pl.pallas_call(
        functools.partial(_gdn_prep_kernel, n_heads=nh),
        out_shape=tuple(one_dir_shapes * 2),
        grid=(t // rb,),
        in_specs=[row_spec(0), row_spec(1), row_spec(2),
                  pl.BlockSpec((rb, LANES), lambda i: (i, 0)),
                  pl.BlockSpec((1, LANES), lambda i: (0, 0)),
                  pl.BlockSpec((1, LANES), lambda i: (0, 0))],
        out_specs=tuple(one_dir_specs * 2),
        compiler_params=_params(("arbitrary",)),
        name="gdn_prep",
    )(qkv, qkv, qkv, ab, alog_row, dtb_row)


def _gdn_scan_kernel(meta_ref, u_ref, w_ref, qg_ref, kdt_ref, at_ref, ab_ref, alog_ref, dtb_ref, *rest,
                     n_heads, reverse, finalize):
    if finalize:
        ofwd_ref, z_ref, nw_ref, o_ref, s_ref = rest
    else:
        o_ref, s_ref = rest
    C = GDN_CHUNK
    n_chunks = u_ref.shape[0] // C
    i = pl.program_id(0)

    @pl.when(meta_ref[0, i] == 1)
    def _():
        s_ref[...] = jnp.zeros_like(s_ref)

    g_all, _ = _gdn_gates(ab_ref[...], alog_ref[...], dtb_ref[...])
    g_col = (3 if reverse else 1) * n_heads
    heads = range(n_heads)
    hs = [slice(h * HEAD_DIM, (h + 1) * HEAD_DIM) for h in heads]
    order = range(n_chunks - 1, -1, -1) if reverse else range(n_chunks)
    for c in order:
        rows = slice(c * C, (c + 1) * C)
        chunk_decay = jnp.exp(jnp.sum(g_all[rows], axis=0, keepdims=True))
        wq = [jnp.dot(jnp.concatenate([w_ref[rows, hs[h]], qg_ref[rows, hs[h]]], axis=0),
                      s_ref[h].astype(BF16), preferred_element_type=F32) for h in heads]
        vn = [(u_ref[rows, hs[h]].astype(F32) - wq[h][:C]).astype(BF16) for h in heads]
        outs = [wq[h][C:] + jnp.dot(at_ref[rows, h * C:(h + 1) * C], vn[h], preferred_element_type=F32)
                for h in heads]
        for h in heads:
            s_ref[h] = (s_ref[h] * chunk_decay[:, g_col + h:g_col + h + 1]
                        + jnp.dot(kdt_ref[c * HEAD_DIM:(c + 1) * HEAD_DIM, h * C:(h + 1) * C], vn[h],
                                  preferred_element_type=F32))
        for h in heads:
            o = outs[h]
            if finalize:
                o = o + ofwd_ref[rows, hs[h]]
                o = o * lax.rsqrt(jnp.mean(o * o, axis=-1, keepdims=True) + EPS) * nw_ref[...]
                z = z_ref[rows, hs[h]].astype(F32)
                o = o * (z * (1.0 / (1.0 + jnp.exp(-z))))
            o_ref[rows, hs[h]] = o.astype(o_ref.dtype)


def gdn_scan(prep, ab, alog_row, dtb_row, meta, gw, reverse, ofwd=None, proj=None, z_col=None, norm_w=None):
    u, w, qg, kdt, attn = prep
    t = u.shape[0]
    nh = gw // HEAD_DIM
    rb = GDN_ROWS
    nb = t // rb
    aw = attn.shape[1]
    finalize = ofwd is not None
    idx = (lambda i, m: (nb - 1 - i)) if reverse else (lambda i, m: i)
    wide = pl.BlockSpec((rb, gw), lambda i, m: (idx(i, m), 0))
    in_specs = [wide, wide, wide,
                pl.BlockSpec((rb // GDN_CHUNK * HEAD_DIM, aw), lambda i, m: (idx(i, m), 0)),
                pl.BlockSpec((rb, aw), lambda i, m: (idx(i, m), 0)),
                pl.BlockSpec((rb, LANES), lambda i, m: (idx(i, m), 0)),
                pl.BlockSpec((1, LANES), lambda i, m: (0, 0)),
                pl.BlockSpec((1, LANES), lambda i, m: (0, 0))]
    args = [meta, u, w, qg, kdt, attn, ab, alog_row, dtb_row]
    if finalize:
        assert z_col % gw == 0
        in_specs += [wide,
                     pl.BlockSpec((rb, gw), lambda i, m: (idx(i, m), z_col // gw)),
                     pl.BlockSpec((1, HEAD_DIM), lambda i, m: (0, 0))]
        args += [ofwd, proj, norm_w]
    return pl.pallas_call(
        functools.partial(_gdn_scan_kernel, n_heads=nh, reverse=reverse, finalize=finalize),
        out_shape=jax.ShapeDtypeStruct((t, gw), BF16 if finalize else F32),
        grid_spec=pltpu.PrefetchScalarGridSpec(
            num_scalar_prefetch=1,
            grid=(nb,),
            in_specs=in_specs,
            out_specs=wide,
            scratch_shapes=[pltpu.VMEM((nh, HEAD_DIM, HEAD_DIM), F32)],
        ),
        compiler_params=_params(("arbitrary",)),
        name="gdn_scan_bwd" if reverse else "gdn_scan_fwd",
    )(*args)


def _outproj_kernel(oa_ref, og_ref, wa_ref, wg_ref, xp_ref, xs_ref, o_ref, *, nbp):
    i = pl.program_id(0)
    mixed = (jnp.dot(oa_ref[...], wa_ref[...], preferred_element_type=F32)
             + jnp.dot(og_ref[...], wg_ref[...], preferred_element_type=F32))

    @pl.when(i < nbp)
    def _():
        o_ref[...] = xp_ref[...] + mixed

    @pl.when(i >= nbp)
    def _():
        o_ref[...] = xs_ref[...] + mixed


def out_projection(oa, og, w_out, xp, xs):
    t, aw = oa.shape
    gw = og.shape[1]
    d = w_out.shape[1]
    tp = xp.shape[0]
    tm = _pick(np.gcd(tp, xs.shape[0]), (1024, 512, 256, 128))
    tn = _pick(d, (512, 256, 128))
    nbp = tp // tm
    assert aw % 16 == 0
    return pl.pallas_call(
        functools.partial(_outproj_kernel, nbp=nbp),
        out_shape=jax.ShapeDtypeStruct((t, d), F32),
        grid=(t // tm, d // tn),
        in_specs=[
            pl.BlockSpec((tm, aw), lambda i, j: (i, 0)),
            pl.BlockSpec((tm, gw), lambda i, j: (i, 0)),
            pl.BlockSpec((aw, tn), lambda i, j: (0, j)),
            pl.BlockSpec((gw, tn), lambda i, j: (aw // gw, j)),
            pl.BlockSpec((tm, tn), lambda i, j: (jnp.minimum(i, nbp - 1), jnp.where(i < nbp, j, 0))),
            pl.BlockSpec((tm, tn), lambda i, j: (jnp.maximum(i - nbp, 0), jnp.where(i >= nbp, j, 0))),
        ],
        out_specs=pl.BlockSpec((tm, tn), lambda i, j: (i, j)),
        compiler_params=_params(("arbitrary", "arbitrary")),
        name="out_proj",
    )(oa, og, w_out, w_out, xp, xs)


def _router_kernel(x_ref, g_ref, wr_ref, br_ref, h_ref, idx_ref, gate_ref):
    h = _rms(x_ref[...], g_ref[...])
    h_ref[...] = h
    logits = jnp.dot(h, wr_ref[...], precision=lax.Precision.HIGHEST,
                     preferred_element_type=F32) + br_ref[...]
    col = lax.broadcasted_iota(I32, logits.shape, 1).astype(F32)
    vals = logits
    tops, idxs = [], []
    for _ in range(TOP_K):
        m = jnp.max(vals, axis=-1, keepdims=True)
        idx = jnp.min(jnp.where(vals == m, col, float(LANES)), axis=-1, keepdims=True)
        tops.append(m)
        idxs.append(idx)
        vals = jnp.where(col == idx, -jnp.inf, vals)
    es = [jnp.exp(v - tops[0]) for v in tops]
    den = es[0]
    for e in es[1:]:
        den = den + e
    idx_out = jnp.zeros(logits.shape, F32)
    gate_out = jnp.zeros(logits.shape, F32)
    for k in range(TOP_K):
        idx_out = jnp.where(col == float(k), idxs[k], idx_out)
        gate_out = jnp.where(col == float(k), es[k] / den, gate_out)
    idx_ref[...] = idx_out.astype(I32)
    gate_ref[...] = gate_out


def ffn_norm_router(x1, g, w_router, b_router):
    t, d = x1.shape
    e = w_router.shape[1]
    assert TOP_K <= e <= LANES
    wr = jnp.zeros((d, LANES), F32).at[:, :e].set(w_router)
    br = jnp.full((1, LANES), NEG_INF, F32).at[0, :e].set(b_router)
    tr = _pick(t, (256, 128))
    return pl.pallas_call(
        _router_kernel,
        out_shape=(jax.ShapeDtypeStruct((t, d), F32),
                   jax.ShapeDtypeStruct((t, LANES), I32),
                   jax.ShapeDtypeStruct((t, LANES), F32)),
        grid=(t // tr,),
        in_specs=[
            pl.BlockSpec((tr, d), lambda i: (i, 0)),
            pl.BlockSpec((1, d), lambda i: (0, 0)),
            pl.BlockSpec((d, LANES), lambda i: (0, 0)),
            pl.BlockSpec((1, LANES), lambda i: (0, 0)),
        ],
        out_specs=(pl.BlockSpec((tr, d), lambda i: (i, 0)),
                   pl.BlockSpec((tr, LANES), lambda i: (i, 0)),
                   pl.BlockSpec((tr, LANES), lambda i: (i, 0))),
        compiler_params=_params(("arbitrary",)),
        name="ffn_norm_router",
    )(x1, g.reshape(1, d), wr, br)


def _gather_rows(tok_ref, src_hbm, dst, sem, n_rows, row0=0):
    def body(j, carry):
        r = row0 + j
        tok = tok_ref[0, 0, r]
        pltpu.make_async_copy(src_hbm.at[pl.ds(tok, 1)], dst.at[pl.ds(r, 1)], sem).start()
        return carry
    lax.fori_loop(0, n_rows, body, 0, unroll=GATHER_UNROLL)


def _wait_rows(src_hbm, dst, sem):
    pltpu.make_async_copy(src_hbm.at[pl.ds(0, dst.shape[0])], dst, sem).wait()


def _moe_up_kernel(be_ref, nv_ref, tokc_ref, tokn_ref, x_hbm, wg_ref, bg_ref, wu_ref, bu_ref, h_ref,
                   xg, xb, sem, *, tm, nt):
    m = pl.program_id(0)
    n = pl.program_id(1)
    nvalid = nv_ref[0]
    live = m < nvalid
    rows_per_step = tm // nt

    @pl.when(live & (n == 0))
    def _():
        @pl.when(m == 0)
        def _():
            _gather_rows(tokc_ref, x_hbm, xg, sem, tm)
        _wait_rows(x_hbm, xg, sem)
        xb[...] = xg[...].astype(BF16)

    @pl.when(live)
    def _():
        row0 = n * rows_per_step
        for j in range(rows_per_step):
            tok = tokn_ref[0, 0, row0 + j]
            pltpu.make_async_copy(x_hbm.at[pl.ds(tok, 1)], xg.at[pl.ds(row0 + j, 1)], sem).start()
        x = xb[...]
        gate = jnp.dot(x, wg_ref[...].astype(BF16), preferred_element_type=F32) + bg_ref[...]
        up = jnp.dot(x, wu_ref[...].astype(BF16), preferred_element_type=F32) + bu_ref[...]
        gate = jnp.minimum(gate, SWIGLU_LIMIT)
        up = jnp.clip(up, -SWIGLU_LIMIT, SWIGLU_LIMIT)
        glu = gate * (1.0 / (1.0 + jnp.exp(-gate * SWIGLU_ALPHA)))
        h_ref[...] = ((up + 1.0) * glu).astype(h_ref.dtype)

    @pl.when((m == nvalid - 1) & (n == nt - 1))
    def _():
        _wait_rows(x_hbm, xg, sem)

    @pl.when(jnp.logical_not(live))
    def _():
        h_ref[...] = jnp.zeros_like(h_ref)


def _moe_up_col_tile(f):
    return _pick(f, (256, 128))


def _moe_row_block(n_assign, n_experts, f):
    mean = n_assign / n_experts
    sd = (mean * (1.0 - 1.0 / n_experts)) ** 0.5
    k = max(1, round(mean / MOE_ROW_BLOCK))
    step = int(np.lcm(BF16_SUBLANES, f // _moe_up_col_tile(f)))
    return int(-(-(mean + 2.0 * sd) // (k * step)) * step)


def moe_up(h2, row_tok3, block_e, nvalid, w_gate, b_gate, w_up, b_up, tm):
    n_blocks = row_tok3.shape[0]
    e, d, f = w_gate.shape
    tn = _moe_up_col_tile(f)
    nt = f // tn
    assert tm % nt == 0
    wmap = lambda m, n, be, nv: (be[m], 0, jnp.where(m < nv[0], n, nt - 1))
    return pl.pallas_call(
        functools.partial(_moe_up_kernel, tm=tm, nt=nt),
        out_shape=jax.ShapeDtypeStruct((n_blocks * tm, f), BF16),
        grid_spec=pltpu.PrefetchScalarGridSpec(
            num_scalar_prefetch=2,
            grid=(n_blocks, nt),
            in_specs=[
                pl.BlockSpec((1, 1, tm), lambda m, n, be, nv: (m, 0, 0), memory_space=pltpu.SMEM),
                pl.BlockSpec((1, 1, tm), lambda m, n, be, nv: (jnp.minimum(m + 1, n_blocks - 1), 0, 0),
                             memory_space=pltpu.SMEM),
                pl.BlockSpec(memory_space=pl.ANY),
                pl.BlockSpec((None, d, tn), wmap),
                pl.BlockSpec((None, 1, tn), wmap),
                pl.BlockSpec((None, d, tn), wmap),
                pl.BlockSpec((None, 1, tn), wmap),
            ],
            out_specs=pl.BlockSpec((tm, tn), lambda m, n, be, nv: (m, n)),
            scratch_shapes=[pltpu.VMEM((tm, d), F32), pltpu.VMEM((tm, d), BF16),
                            pltpu.SemaphoreType.DMA(())],
        ),
        compiler_params=_params(("arbitrary", "arbitrary")),
        name="moe_up",
    )(block_e, nvalid, row_tok3, row_tok3, h2, w_gate, b_gate.reshape(e, 1, f), w_up, b_up.reshape(e, 1, f))


def _moe_down_kernel(be_ref, nv_ref, h_ref, wd_ref, bd_ref, o_ref):
    m = pl.program_id(0)
    live = m < nv_ref[0]

    @pl.when(live)
    def _():
        o = jnp.dot(h_ref[...], wd_ref[...].astype(BF16), preferred_element_type=F32) + bd_ref[...]
        o_ref[...] = _pack_bf16_pair(o[:, :MOE_PACK_COLS], o[:, MOE_PACK_COLS:])

    @pl.when(jnp.logical_not(live))
    def _():
        o_ref[...] = jnp.zeros_like(o_ref)


MOE_PACK_COLS = 256


def _pack_bf16_pair(lo, hi):
    bits = lambda a: lax.bitcast_convert_type(a.astype(BF16).astype(F32), jnp.uint32)
    return (bits(lo) >> 16) | (bits(hi) & jnp.uint32(0xFFFF0000))


def _unpack_bf16_pair(w):
    lo = lax.bitcast_convert_type(w << 16, F32)
    hi = lax.bitcast_convert_type(w & jnp.uint32(0xFFFF0000), F32)
    return lo, hi


def moe_down(hmid, block_e, nvalid, w_down, b_down, tm):
    n_rows, f = hmid.shape
    e, _, d = w_down.shape
    n_blocks = n_rows // tm
    tn = 2 * MOE_PACK_COLS
    assert d % tn == 0
    nt = d // tn
    wmap = lambda m, n, be, nv: (be[m], 0, jnp.where(m < nv[0], n, nt - 1))
    return pl.pallas_call(
        _moe_down_kernel,
        out_shape=jax.ShapeDtypeStruct((n_rows, d // 2), jnp.uint32),
        grid_spec=pltpu.PrefetchScalarGridSpec(
            num_scalar_prefetch=2,
            grid=(n_blocks, nt),
            in_specs=[
                pl.BlockSpec((tm, f), lambda m, n, be, nv: (m, 0)),
                pl.BlockSpec((None, f, tn), wmap),
                pl.BlockSpec((None, 1, tn), wmap),
            ],
            out_specs=pl.BlockSpec((tm, MOE_PACK_COLS), lambda m, n, be, nv: (m, n)),
        ),
        compiler_params=_params(("arbitrary", "arbitrary")),
        name="moe_down",
    )(block_e, nvalid, hmid, w_down, b_down.reshape(e, 1, d))


def _combine_kernel(destc_ref, destn_ref, rows_hbm, x_ref, gate_ref, g_ref, o_ref, buf, sem, *, tr, nb):
    i = pl.program_id(0)
    slot = i % 2
    n_rows = TOP_K * tr

    @pl.when(i == 0)
    def _():
        _gather_rows(destc_ref, rows_hbm, buf.at[0], sem.at[0], n_rows)

    @pl.when(i + 1 < nb)
    def _():
        _gather_rows(destn_ref, rows_hbm, buf.at[1 - slot], sem.at[1 - slot], n_rows)

    _wait_rows(rows_hbm, buf.at[slot], sem.at[slot])
    gates = gate_ref[...]
    P = MOE_PACK_COLS
    parts = []
    for j in range(x_ref.shape[1] // (2 * P)):
        lo = x_ref[:, 2 * j * P:(2 * j + 1) * P]
        hi = x_ref[:, (2 * j + 1) * P:(2 * j + 2) * P]
        for k in range(TOP_K):
            wlo, whi = _unpack_bf16_pair(buf[slot, pl.ds(k * tr, tr), j * P:(j + 1) * P])
            lo = lo + gates[:, k:k + 1] * wlo
            hi = hi + gates[:, k:k + 1] * whi
        parts += [lo, hi]
    o_ref[...] = _rms(jnp.concatenate(parts, axis=1), g_ref[...])


def moe_combine_norm(x1, rows, dest3, gates, g, row0, n_tok, tr):
    d = x1.shape[1]
    nb = n_tok // tr
    b0 = row0 // tr
    nbt = dest3.shape[0]
    return pl.pallas_call(
        functools.partial(_combine_kernel, tr=tr, nb=nb),
        out_shape=jax.ShapeDtypeStruct((n_tok, d), F32),
        grid=(nb,),
        in_specs=[
            pl.BlockSpec((1, 1, TOP_K * tr), lambda i: (b0 + i, 0, 0), memory_space=pltpu.SMEM),
            pl.BlockSpec((1, 1, TOP_K * tr), lambda i: (jnp.minimum(b0 + i + 1, nbt - 1), 0, 0),
                         memory_space=pltpu.SMEM),
            pl.BlockSpec(memory_space=pl.ANY),
            pl.BlockSpec((tr, d), lambda i: (b0 + i, 0)),
            pl.BlockSpec((tr, LANES), lambda i: (b0 + i, 0)),
            pl.BlockSpec((1, d), lambda i: (0, 0)),
        ],
        out_specs=pl.BlockSpec((tr, d), lambda i: (i, 0)),
        scratch_shapes=[pltpu.VMEM((2, TOP_K * tr, d // 2), jnp.uint32), pltpu.SemaphoreType.DMA((2,))],
        compiler_params=_params(("arbitrary",)),
        name="moe_combine_norm",
    )(dest3, dest3, rows, x1, gates, g.reshape(1, d))


def _route(top_idx, n_experts, tm):
    t = top_idx.shape[0]
    n_assign = t * TOP_K
    flat_e = top_idx.reshape(-1)
    onehot = (flat_e[:, None] == jnp.arange(n_experts, dtype=I32)[None, :]).astype(I32)
    csum = jnp.cumsum(onehot, axis=0)
    counts = csum[-1]
    rank = jnp.take_along_axis(csum, flat_e[:, None], axis=1)[:, 0] - 1
    padded = ((counts + tm - 1) // tm) * tm
    pend = jnp.cumsum(padded)
    pstart = pend - padded
    dest = pstart[flat_e] + rank
    n_blocks = -(-n_assign // tm) + n_experts
    flat_tok = jnp.arange(n_assign, dtype=I32) // TOP_K
    row_tok = jnp.zeros((n_blocks * tm,), I32).at[dest].set(flat_tok, unique_indices=True)
    block_e = jnp.minimum(
        jnp.searchsorted(pend, jnp.arange(n_blocks, dtype=I32) * tm, side='right'), n_experts - 1).astype(I32)
    nvalid = (pend[-1] // tm).astype(I32).reshape(1)
    return row_tok, dest.reshape(t, TOP_K).astype(I32), block_e, nvalid


def _seq_table(seqs, blk):
    pos0, slen, first, last = [], [], [], []
    for (b, s) in seqs:
        assert s % blk == 0
        nb = s // blk
        for _ in range(b):
            for j in range(nb):
                pos0.append(j * blk)
                slen.append(s)
                first.append(int(j == 0))
                last.append(int(j == nb - 1))
    return np.array(pos0, np.int32), np.array(slen, np.int32), np.array(first, np.int32), np.array(last, np.int32)


def _rope_tables(seqs):
    half = HEAD_DIM // 2
    inv = ROPE_THETA ** (-jnp.arange(0, HEAD_DIM, 2, dtype=F32) / HEAD_DIM)
    cos_l, sin_l = [], []
    for (b, s) in seqs:
        ang = jnp.arange(s, dtype=F32)[:, None] * inv[None, :]
        c = jnp.cos(ang)
        sn = jnp.sin(ang)
        cos_l.append(jnp.tile(jnp.concatenate([c, c], axis=-1), (b, 1)))
        sin_l.append(jnp.tile(jnp.concatenate([-sn, sn], axis=-1), (b, 1)))
    del half
    return jnp.concatenate(cos_l, axis=0), jnp.concatenate(sin_l, axis=0)


def _layer(xp, xs, seqs, norm_mix, w_in, conv_w, a_log_fwd, dt_bias_fwd, a_log_bwd, dt_bias_bwd,
           gdn_norm, w_out, norm_ffn, w_router, b_router, w_gate, b_gate, w_up, b_up, w_down, b_down,
           norm_final):
    comb_tr, attn_tq = COMBINE_TOKEN_BLOCK, ATTN_Q_BLOCK
    d = xp.shape[1]
    nhg = a_log_fwd.shape[0]
    gw = nhg * HEAD_DIM
    aw = w_out.shape[0] - gw
    n_main = 3 * aw + 4 * gw
    assert w_in.shape[1] == n_main + 4 * nhg and 4 * nhg <= LANES
    t = xp.shape[0] + xs.shape[0]

    h = norm_two_streams(xp, xs, norm_mix)
    w_main = w_in[:, :n_main].astype(BF16)
    w_tail = jnp.zeros((d, LANES), BF16).at[:, :4 * nhg].set(w_in[:, n_main:].astype(BF16))
    proj, ab = in_projection(h, w_main, w_tail)

    cos, sin = _rope_tables(seqs)
    qk = rope_qk(proj, cos, sin, aw)
    pos0, slen, _, _ = _seq_table(seqs, attn_tq)
    o_attn = dilated_attention(qk, proj, jnp.asarray(np.stack([pos0, slen])), aw, attn_tq)

    conv_tb = _pick(t, (512, 256, 128))
    _, _, cfirst, clast = _seq_table(seqs, conv_tb)
    qkv = gdn_conv(proj, conv_w, jnp.asarray(np.stack([cfirst, clast])), 3 * aw, gw)
    _, _, gfirst, glast = _seq_table(seqs, LANES)
    zrow = jnp.zeros((nhg,), F32)
    alog_row = jnp.zeros((1, LANES), F32).at[0, :4 * nhg].set(
        jnp.concatenate([zrow, a_log_fwd, zrow, a_log_bwd]))
    dtb_row = jnp.zeros((1, LANES), F32).at[0, :4 * nhg].set(
        jnp.concatenate([zrow, dt_bias_fwd, zrow, dt_bias_bwd]))
    prep = gdn_prep(qkv, ab, alog_row, dtb_row, gw)
    o_fwd = gdn_scan(prep[:5], ab, alog_row, dtb_row, jnp.asarray(gfirst[None, :]), gw, reverse=False)
    o_gdn = gdn_scan(prep[5:], ab, alog_row, dtb_row, jnp.asarray(glast[::-1][None, :].copy()), gw, reverse=True,
                     ofwd=o_fwd, proj=proj, z_col=3 * aw + 3 * gw, norm_w=gdn_norm.reshape(1, HEAD_DIM))

    x1 = out_projection(o_attn, o_gdn, w_out.astype(BF16), xp, xs)

    h2, idx_pad, gate_pad = ffn_norm_router(x1, norm_ffn, w_router, b_router)
    n_experts = w_router.shape[1]
    moe_tm = _moe_row_block(t * TOP_K, n_experts, w_gate.shape[2])
    row_tok, dest, block_e, nvalid = _route(idx_pad[:, :TOP_K], n_experts, moe_tm)
    n_blocks = block_e.shape[0]
    hmid = moe_up(h2, row_tok.reshape(n_blocks, 1, moe_tm), block_e, nvalid, w_gate, b_gate, w_up, b_up, moe_tm)
    rows = moe_down(hmid, block_e, nvalid, w_down, b_down, moe_tm)
    dest3 = dest.reshape(t // comb_tr, 1, comb_tr, TOP_K).transpose(0, 1, 3, 2).reshape(t // comb_tr, 1, TOP_K * comb_tr)
    tp = xp.shape[0]
    yp = moe_combine_norm(x1, rows, dest3, gate_pad, norm_final, 0, tp, comb_tr)
    ys = moe_combine_norm(x1, rows, dest3, gate_pad, norm_final, tp, xs.shape[0], comb_tr)
    return yp, ys


def kernel(x_prompt, x_sample, norm_mix, w_in, conv_w, a_log_fwd, dt_bias_fwd, a_log_bwd, dt_bias_bwd,
           gdn_norm, w_out, norm_ffn, w_router, b_router, w_gate, b_gate, w_up, b_up, w_down, b_down,
           norm_final):
    assert norm_mix.shape[0] == 1, "single-layer encoder"
    bp, sp, d = x_prompt.shape
    bs, ss, _ = x_sample.shape
    seqs = ((bp, sp), (bs, ss))
    layer0 = lambda a: a.reshape(a.shape[1:])
    yp, ys = _layer(
        x_prompt.reshape(bp * sp, d), x_sample.reshape(bs * ss, d), seqs,
        *[layer0(a) for a in (norm_mix, w_in, conv_w, a_log_fwd, dt_bias_fwd, a_log_bwd, dt_bias_bwd,
                              gdn_norm, w_out, norm_ffn, w_router, b_router, w_gate, b_gate, w_up,
                              b_up, w_down, b_down)],
        norm_final)
    return yp.reshape(bp, sp, d), ys.reshape(bs, ss, d)
```
